```python
import jax, jax.numpy as jnp
from jax import lax
import numpy as np

D_MODEL = 1024
BATCH = 8
SEQ = 4096
DEPTH = 2
DEC_BATCH = 8
DEC_SEQ = 16
PAST_LEN = 2048

CHUNK = 64
Q_BLOCK = 128
HEAD_DIM = 64
FOX_HEADS = 4
SB_HEADS = 4
MLA_HEADS = 8
FOX_WIDTH = FOX_HEADS * HEAD_DIM
SB_WIDTH = SB_HEADS * HEAD_DIM
MLA_NOPE = 64
MLA_ROPE = 32
MLA_V = 64
MLA_WIDTH = MLA_HEADS * MLA_V
Q_LORA = 256
KV_LORA = 128
MIX_WIDTH = FOX_WIDTH + SB_WIDTH + MLA_WIDTH
IN_SPLITS = (FOX_WIDTH, FOX_WIDTH, FOX_WIDTH, FOX_HEADS, FOX_WIDTH,
             SB_WIDTH, SB_WIDTH, SB_WIDTH, SB_WIDTH,
             Q_LORA, KV_LORA, MLA_ROPE, MLA_WIDTH)
IN_WIDTH = sum(IN_SPLITS)
ROPE_THETA = 10000.0
EPS = 1e-6

kernel_name = 'hybrid_fox_sb_mla_stream_step'


def rmsnorm(x, g):
    xf = x.astype(jnp.float32)
    y = xf * lax.rsqrt(jnp.mean(xf * xf, axis=-1, keepdims=True) + EPS)
    return (y * g.astype(jnp.float32)).astype(x.dtype)


def rope(x, pos):
    half = MLA_ROPE // 2
    inv = ROPE_THETA ** (-jnp.arange(half, dtype=jnp.float32) / half)
    ang = pos.astype(jnp.float32)[:, None] * inv[None, :]
    cos = jnp.cos(ang)[None, :, None, :]
    sin = jnp.sin(ang)[None, :, None, :]
    xf = x.astype(jnp.float32)
    x1, x2 = xf[..., :half], xf[..., half:]
    return jnp.concatenate([x1 * cos - x2 * sin, x2 * cos + x1 * sin], axis=-1).astype(x.dtype)


def fox_attend(q, fq, k, v, fk, q_pos, k_pos):
    s = jnp.einsum('bqhd,bkhd->bhqk', q, k).astype(jnp.float32) * HEAD_DIM ** -0.5
    s = s + jnp.transpose(fq, (0, 2, 1))[..., None] - jnp.transpose(fk, (0, 2, 1))[:, :, None, :]
    mask = k_pos[None, :] <= q_pos[:, None]
    p = jax.nn.softmax(jnp.where(mask, s, -jnp.inf), axis=-1)
    return jnp.einsum('bhqk,bkhd->bqhd', p.astype(v.dtype), v)


def sb_attend(q, k, v, q_pos, k_pos):
    z = jnp.einsum('bqhd,bkhd->bhqk', q, k).astype(jnp.float32) * HEAD_DIM ** -0.5
    mask = k_pos[None, :] < q_pos[:, None]
    log_1m = jnp.where(mask, jax.nn.log_sigmoid(-z), 0.0)
    between = lax.cumsum(log_1m, axis=3, reverse=True) - log_1m
    a = jnp.where(mask, jnp.exp(jax.nn.log_sigmoid(z) + between), 0.0)
    return jnp.einsum('bhqk,bkhd->bqhd', a.astype(v.dtype), v)


def mla_attend(q_lat, q_rope, ckv, kpe, q_pos, k_pos):
    s = (jnp.einsum('bqhc,bkc->bhqk', q_lat, ckv)
         + jnp.einsum('bqhr,bkr->bhqk', q_rope, kpe)).astype(jnp.float32) * (MLA_NOPE + MLA_ROPE) ** -0.5
    mask = (k_pos[None, :] // CHUNK) <= (q_pos[:, None] // CHUNK)
    p = jax.nn.softmax(jnp.where(mask, s, -jnp.inf), axis=-1)
    return jnp.einsum('bhqk,bkc->bqhc', p.astype(ckv.dtype), ckv)


def sweep(attend, q_args, k_args, q_pos, k_pos):
    n_q = q_pos.shape[0]
    past = k_pos.shape[0] - n_q
    outs = []
    for start in range(0, n_q, Q_BLOCK):
        stop = min(start + Q_BLOCK, n_q)
        kend = past + stop
        outs.append(attend(*[a[:, start:stop] for a in q_args], *[a[:, :kend] for a in k_args],
                           q_pos[start:stop], k_pos[:kend]))
    return jnp.concatenate(outs, axis=1)


def layer(x, c, past, g_pre, g_post, w_ada, b_ada, w_in, b_f, g_q_a, w_uq, g_kv_a, w_uk, w_uv, w_out):
    b, s, _ = x.shape
    n_past = 0 if past is None else past[0].shape[1]
    q_pos = n_past + jnp.arange(s, dtype=jnp.int32)
    k_pos = jnp.arange(n_past + s, dtype=jnp.int32)
    mod = jax.nn.silu(c) @ w_ada + b_ada
    shift, scale, gate = jnp.split(mod, 3, axis=-1)
    h = rmsnorm(x, g_pre) * (1 + scale[:, None]) + shift[:, None]
    parts = jnp.split(h @ w_in, np.cumsum(IN_SPLITS)[:-1].tolist(), axis=-1)
    fq, fk, fv, ff, fg, sq, sk, sv, sg, cq, ckv, kpe, mg = parts
    fq = fq.reshape(b, s, FOX_HEADS, HEAD_DIM)
    fk = fk.reshape(b, s, FOX_HEADS, HEAD_DIM)
    fv = fv.reshape(b, s, FOX_HEADS, HEAD_DIM)
    logf = jax.nn.log_sigmoid((ff + b_f).astype(jnp.float32))
    sq = sq.reshape(b, s, SB_HEADS, HEAD_DIM)
    sk = sk.reshape(b, s, SB_HEADS, HEAD_DIM)
    sv = sv.reshape(b, s, SB_HEADS, HEAD_DIM)
    q = (rmsnorm(cq, g_q_a) @ w_uq).reshape(b, s, MLA_HEADS, MLA_NOPE + MLA_ROPE)
    q_rope = rope(q[..., MLA_NOPE:], q_pos)
    q_lat = jnp.einsum('bshn,chn->bshc', q[..., :MLA_NOPE], w_uk)
    ckv = rmsnorm(ckv, g_kv_a)
    kpe = rope(kpe[:, :, None, :], q_pos)[:, :, 0]
    new = (fk, fv, logf, sk, sv, ckv, kpe)
    if past is None:
        full = new
    else:
        full = tuple(jnp.concatenate([p_, n_], axis=1) for p_, n_ in zip(past, new))
    k_fk, k_fv, k_logf, k_sk, k_sv, k_ckv, k_kpe = full
    F = jnp.cumsum(k_logf.astype(jnp.float32), axis=1)
    o_fox = sweep(fox_attend, (fq, F[:, n_past:]), (k_fk, k_fv, F), q_pos, k_pos)
    o_sb = sweep(sb_attend, (sq,), (k_sk, k_sv), q_pos, k_pos)
    o_lat = sweep(mla_attend, (q_lat, q_rope), (k_ckv, k_kpe), q_pos, k_pos)
    o_mla = jnp.einsum('bshc,chv->bshv', o_lat, w_uv)
    y = jnp.concatenate([o_fox.reshape(b, s, FOX_WIDTH) * jax.nn.silu(fg),
                         o_sb.reshape(b, s, SB_WIDTH) * jax.nn.silu(sg),
                         o_mla.reshape(b, s, MLA_WIDTH) * jax.nn.silu(mg)], axis=-1) @ w_out
    x = x + gate[:, None] * rmsnorm(y, g_post)
    return x, new


def setup_inputs(seed: int = 0) -> dict:
    key = jax.random.key(seed)
    ks = jax.random.split(key, 32)
    nrm = jax.random.normal
    f32 = jnp.float32
    d = D_MODEL
    return {
        'x_prompt': nrm(ks[0], (BATCH, SEQ, d), f32),
        'x_sample': nrm(ks[1], (DEC_BATCH, DEC_SEQ, d), f32),
        'c_prompt': nrm(ks[2], (BATCH, d), f32),
        'c_sample': nrm(ks[3], (DEC_BATCH, d), f32),
        'cache_fox_k': nrm(ks[4], (DEPTH, DEC_BATCH, PAST_LEN, FOX_HEADS, HEAD_DIM), f32),
        'cache_fox_v': nrm(ks[5], (DEPTH, DEC_BATCH, PAST_LEN, FOX_HEADS, HEAD_DIM), f32),
        'cache_fox_logf': jax.nn.log_sigmoid(nrm(ks[6], (DEPTH, DEC_BATCH, PAST_LEN, FOX_HEADS), f32)),
        'cache_sb_k': nrm(ks[7], (DEPTH, DEC_BATCH, PAST_LEN, SB_HEADS, HEAD_DIM), f32),
        'cache_sb_v': nrm(ks[8], (DEPTH, DEC_BATCH, PAST_LEN, SB_HEADS, HEAD_DIM), f32),
        'cache_mla_ckv': nrm(ks[9], (DEPTH, DEC_BATCH, PAST_LEN, KV_LORA), f32),
        'cache_mla_kpe': nrm(ks[10], (DEPTH, DEC_BATCH, PAST_LEN, MLA_ROPE), f32),
        'g_pre': 1.0 + 0.05 * nrm(ks[11], (DEPTH, d), f32),
        'g_post': 1.0 + 0.05 * nrm(ks[12], (DEPTH, d), f32),
        'w_ada': 0.5 * nrm(ks[13], (DEPTH, d, 3 * d), f32) * d ** -0.5,
        'b_ada': 0.01 * nrm(ks[14], (DEPTH, 3 * d), f32),
        'w_in': nrm(ks[15], (DEPTH, d, IN_WIDTH), f32) * d ** -0.5,
        'b_f': 0.5 * nrm(ks[16], (DEPTH, FOX_HEADS), f32),
        'g_q_a': 1.0 + 0.05 * nrm(ks[17], (DEPTH, Q_LORA), f32),
        'w_uq': nrm(ks[18], (DEPTH, Q_LORA, MLA_HEADS * (MLA_NOPE + MLA_ROPE)), f32) * Q_LORA ** -0.5,
        'g_kv_a': 1.0 + 0.05 * nrm(ks[19], (DEPTH, KV_LORA), f32),
        'w_uk': nrm(ks[20], (DEPTH, KV_LORA, MLA_HEADS, MLA_NOPE), f32) * KV_LORA ** -0.5,
        'w_uv': nrm(ks[21], (DEPTH, KV_LORA, MLA_HEADS, MLA_V), f32) * KV_LORA ** -0.5,
        'w_out': nrm(ks[22], (DEPTH, MIX_WIDTH, d), f32) * MIX_WIDTH ** -0.5,
    }


def reference(x_prompt, x_sample, c_prompt, c_sample, cache_fox_k, cache_fox_v, cache_fox_logf,
              cache_sb_k, cache_sb_v, cache_mla_ckv, cache_mla_kpe,
              g_pre, g_post, w_ada, b_ada, w_in, b_f, g_q_a, w_uq, g_kv_a, w_uk, w_uv, w_out):
    y_prompt = x_prompt
    y_sample = x_sample
    rows_p = []
    rows_s = []
    for l in range(DEPTH):
        w = (g_pre[l], g_post[l], w_ada[l], b_ada[l], w_in[l], b_f[l], g_q_a[l], w_uq[l],
             g_kv_a[l], w_uk[l], w_uv[l], w_out[l])
        y_prompt, new_p = layer(y_prompt, c_prompt, None, *w)
        past = (cache_fox_k[l], cache_fox_v[l], cache_fox_logf[l], cache_sb_k[l], cache_sb_v[l],
                cache_mla_ckv[l], cache_mla_kpe[l])
        y_sample, new_s = layer(y_sample, c_sample, past, *w)
        rows_p.append(new_p)
        rows_s.append(new_s)
    p_fox_k, p_fox_v, p_fox_logf, p_sb_k, p_sb_v, p_mla_ckv, p_mla_kpe = [jnp.stack(t) for t in zip(*rows_p)]
    s_fox_k, s_fox_v, s_fox_logf, s_sb_k, s_sb_v, s_mla_ckv, s_mla_kpe = [jnp.stack(t) for t in zip(*rows_s)]
    return (y_prompt, y_sample,
            p_fox_k, p_fox_v, p_fox_logf, p_sb_k, p_sb_v, p_mla_ckv, p_mla_kpe,
            s_fox_k, s_fox_v, s_fox_logf, s_sb_k, s_sb_v, s_mla_ckv, s_mla_kpe)
```

```python
import functools

import numpy as np
import jax
import jax.numpy as jnp
from jax import lax
from jax.experimental import pallas as pl
from jax.experimental.pallas import tpu as pltpu

D_MODEL = 1024
DEPTH = 2
CHUNK_SHIFT = 6
HEAD_DIM = 64
N_FS_HEADS = 4
FS_WIDTH = N_FS_HEADS * HEAD_DIM
MLA_HEADS = 8
MLA_NOPE = 64
MLA_ROPE = 32
MLA_V = 64
MLA_WIDTH = MLA_HEADS * MLA_V
Q_LORA = 256
KV_LORA = 128
ROPE_THETA = 10000.0
EPS = 1e-6
IN_SPLITS = (256, 256, 256, 4, 256, 256, 256, 256, 256, Q_LORA, KV_LORA, MLA_ROPE, MLA_WIDTH)
IN_OFFSETS = tuple(int(v) for v in np.cumsum((0,) + IN_SPLITS))

LANES = 128
SB_SUB = 256
W_MAIN = 3072
NEG = -1e30
FS_SCALE = HEAD_DIM ** -0.5
MLA_SCALE = (MLA_NOPE + MLA_ROPE) ** -0.5
VMEM_LIMIT = 56 * 1024 * 1024

BF16 = jnp.bfloat16
F32 = jnp.float32


def _dot(a, b):
    return jnp.dot(a, b, preferred_element_type=F32)


def _dot_nt(a, b):
    return lax.dot_general(a, b, (((1,), (1,)), ((), ())), preferred_element_type=F32)


def _silu(g):
    return g / (1.0 + jnp.exp(-g))


def _rms(x, g):
    return x * lax.rsqrt(jnp.mean(x * x, axis=-1, keepdims=True) + EPS) * g


def _mod_kernel(c_ref, w_ref, b_ref, o_ref):
    a = _silu(c_ref[...]).astype(BF16)
    o_ref[0] = _dot(a, w_ref[0].astype(BF16)) + b_ref[0]


def _modulation(c_all, w_ada, b_ada):
    n = c_all.shape[0]
    d = D_MODEL
    return pl.pallas_call(
        _mod_kernel,
        out_shape=jax.ShapeDtypeStruct((DEPTH, n, 3 * d), F32),
        grid=(DEPTH, 3),
        in_specs=[
            pl.BlockSpec((n, d), lambda l, j: (0, 0)),
            pl.BlockSpec((1, d, d), lambda l, j: (l, 0, j)),
            pl.BlockSpec((1, 1, d), lambda l, j: (l, 0, j)),
        ],
        out_specs=pl.BlockSpec((1, n, d), lambda l, j: (l, 0, j)),
        compiler_params=pltpu.CompilerParams(
            dimension_semantics=("arbitrary", "arbitrary"), vmem_limit_bytes=VMEM_LIMIT),
        name="modulation",
    )(c_all, w_ada, b_ada.reshape(DEPTH, 1, 3 * d))


def _proj_kernel(x_ref, mod_ref, gpre_ref, w_ref, bf_ref, gq_ref, wn_ref, wxy_ref, wk_ref,
                 gkv_ref, cos_ref, sin_ref,
                 fq_b, fk_f, fk_b, fv_f, fv_b, fg_f, sq_b, sk_f, sk_b, sv_f, sv_b, sg_f,
                 qcat_b, ckv_f, kcat_b, mg_f, gout_f):
    d = D_MODEL
    x = x_ref[0]
    mod = mod_ref[0]
    shift = mod[:, 0:d]
    scale = mod[:, d:2 * d]
    h = _rms(x, gpre_ref[...]) * (1.0 + scale) + shift
    hb = h.astype(BF16)

    def proj(a, b):
        return _dot(hb, w_ref[:, a:b])

    fq_b[0] = (proj(0, 256) * FS_SCALE).astype(BF16)
    t = proj(256, 512)
    fk_f[0] = t
    fk_b[0] = t.astype(BF16)
    t = proj(512, 768)
    fv_f[0] = t
    fv_b[0] = t.astype(BF16)
    fg_f[0] = proj(768, 1024)
    sq_b[0] = (proj(1024, 1280) * FS_SCALE).astype(BF16)
    t = proj(1280, 1536)
    sk_f[0] = t
    sk_b[0] = t.astype(BF16)
    t = proj(1536, 1792)
    sv_f[0] = t
    sv_b[0] = t.astype(BF16)
    sg_f[0] = proj(1792, 2048)
    mg_f[0] = proj(2560, 3072)

    cos = cos_ref[...]
    sin = sin_ref[...]

    cg = proj(2304, 2560)
    ckvn = _rms(cg[:, 0:KV_LORA], gkv_ref[...])
    ckv_f[0] = ckvn
    grp = cg[:, KV_LORA:2 * KV_LORA]
    rope_k = grp * cos + pltpu.roll(grp, 64, axis=1) * sin
    zf = grp + bf_ref[...]
    logf = jnp.minimum(zf, 0.0) - jnp.log(1.0 + jnp.exp(-jnp.abs(zf)))
    lane = lax.broadcasted_iota(jnp.int32, grp.shape, 1)
    gout_f[0] = jnp.where(lane < MLA_ROPE, rope_k, logf)
    kcat_b[0] = jnp.concatenate([ckvn, rope_k], axis=1).astype(BF16)

    cqn = _rms(proj(2048, 2304), gq_ref[...]).astype(BF16)
    qn = _dot(cqn, wn_ref[...]).astype(BF16)
    for hh in range(MLA_HEADS):
        pair = qn[:, (hh // 2) * LANES:(hh // 2 + 1) * LANES]
        qlat = _dot(pair, wk_ref[hh])
        xy = _dot(cqn, wxy_ref[hh])
        rope_q = xy[:, 0:LANES] * cos + xy[:, LANES:2 * LANES] * sin
        qcat_b[0, hh] = jnp.concatenate([qlat, rope_q], axis=1).astype(BF16)


def _projection(x, mod_rows, cos, sin, lw, tm):
    nb, s, d = x.shape
    mrows = mod_rows.shape[1]
    mblk = 1 if mrows == 1 else tm
    grid = (nb, s // tm)

    def row(width, dtype):
        return (jax.ShapeDtypeStruct((nb, s, width), dtype),
                pl.BlockSpec((1, tm, width), lambda b, i: (b, i, 0)))

    outs = [row(256, BF16), row(256, F32), row(256, BF16), row(256, F32), row(256, BF16),
            row(256, F32), row(256, BF16), row(256, F32), row(256, BF16), row(256, F32),
            row(256, BF16), row(256, F32),
            (jax.ShapeDtypeStruct((nb, MLA_HEADS, s, 256), BF16),
             pl.BlockSpec((1, MLA_HEADS, tm, 256), lambda b, i: (b, 0, i, 0))),
            row(KV_LORA, F32), row(256, BF16), row(MLA_WIDTH, F32), row(LANES, F32)]

    def full(a):
        nd = a.ndim
        return pl.BlockSpec(a.shape, lambda b, i: (0,) * nd)

    consts = [lw["g_pre"], lw["w_main"], lw["bf_row"], lw["g_q"], lw["w_nope"], lw["w_xy"],
              lw["w_k"], lw["g_kv"]]
    in_specs = ([pl.BlockSpec((1, tm, d), lambda b, i: (b, i, 0)),
                 pl.BlockSpec((1, mblk, 3 * d),
                              (lambda b, i: (b, 0, 0)) if mrows == 1 else (lambda b, i: (b, i, 0)))]
                + [full(a) for a in consts]
                + [pl.BlockSpec((tm, LANES), lambda b, i: (i, 0)),
                   pl.BlockSpec((tm, LANES), lambda b, i: (i, 0))])
    return pl.pallas_call(
        _proj_kernel,
        out_shape=[o[0] for o in outs],
        grid=grid,
        in_specs=in_specs,
        out_specs=[o[1] for o in outs],
        compiler_params=pltpu.CompilerParams(
            dimension_semantics=("parallel", "arbitrary"), vmem_limit_bytes=VMEM_LIMIT),
        name="projection",
    )(x, mod_rows, *consts, cos, sin)


def _cumsum_kernel(x_ref, u_ref, l_ref, o_ref):
    x = x_ref[0]
    w = jnp.dot(x, u_ref[...], preferred_element_type=F32, precision=lax.Precision.HIGHEST)
    tot = jnp.broadcast_to(w[:, LANES - 1:LANES], w.shape)
    off = jnp.dot(l_ref[...], tot, preferred_element_type=F32, precision=lax.Precision.HIGHEST)
    o_ref[0] = w + off


def _cumsum_rows(x):
    r, length = x.shape
    n = length // LANES
    u = jnp.asarray(np.triu(np.ones((LANES, LANES), np.float32)))
    lo = jnp.asarray(np.tril(np.ones((n, n), np.float32), -1))
    out = pl.pallas_call(
        _cumsum_kernel,
        out_shape=jax.ShapeDtypeStruct((r, n, LANES), F32),
        grid=(r,),
        in_specs=[pl.BlockSpec((1, n, LANES), lambda i: (i, 0, 0)),
                  pl.BlockSpec((LANES, LANES), lambda i: (0, 0)),
                  pl.BlockSpec((n, n), lambda i: (0, 0))],
        out_specs=pl.BlockSpec((1, n, LANES), lambda i: (i, 0, 0)),
        compiler_params=pltpu.CompilerParams(dimension_semantics=("arbitrary",)),
        name="cumsum_logf",
    )(x.reshape(r, n, LANES), u, lo)
    return out.reshape(r, length)


def _schedule(nq, kmax_fn, reverse):
    qi_l, kj_l, first_l, last_l = [], [], [], []
    for qi in range(nq):
        ks = list(range(kmax_fn(qi) + 1))
        if reverse:
            ks = ks[::-1]
        for n, kj in enumerate(ks):
            qi_l.append(qi)
            kj_l.append(kj)
            first_l.append(int(n == 0))
            last_l.append(int(n == len(ks) - 1))
    return tuple(jnp.asarray(np.asarray(a, np.int32)) for a in (qi_l, kj_l, first_l, last_l))


def _head_lane_mask(shape, h):
    lane = lax.broadcasted_iota(jnp.int32, shape, 1)
    return (lane >= HEAD_DIM * h) & (lane < HEAD_DIM * (h + 1))


def _fox_kernel(qt, kt, ft, lt, q_ref, k_ref, v_ref, fc_ref, fr_ref, g_ref, o_ref,
                qm_sc, m_sc, l_sc, acc_sc, *, past, tq, tk):
    step = pl.program_id(1)
    qi = qt[step]
    kj = kt[step]

    @pl.when(ft[step] == 1)
    def _():
        q = q_ref[0]
        for h in range(N_FS_HEADS):
            qm_sc[h] = jnp.where(_head_lane_mask(q.shape, h), q, jnp.zeros_like(q))
        m_sc[...] = jnp.full(m_sc.shape, NEG, F32)
        l_sc[...] = jnp.zeros(l_sc.shape, F32)
        acc_sc[...] = jnp.zeros(acc_sc.shape, F32)

    k = k_ref[0]
    v = v_ref[0]
    q_pos = past + qi * tq + lax.broadcasted_iota(jnp.int32, (tq, 1), 0)
    k_pos = kj * tk + lax.broadcasted_iota(jnp.int32, (1, tk), 1)
    mask = k_pos <= q_pos
    fc = fc_ref[0]
    fr = fr_ref[0]
    for h in range(N_FS_HEADS):
        s = _dot_nt(qm_sc[h], k)
        s = s + (fc[:, h:h + 1] - fr[h:h + 1, :])
        s = jnp.where(mask, s, NEG)
        m_prev = m_sc[h]
        m_new = jnp.maximum(m_prev, jnp.max(s, axis=1, keepdims=True))
        p = jnp.exp(s - m_new)
        alpha = jnp.exp(m_prev - m_new)
        l_sc[h] = alpha * l_sc[h] + jnp.sum(p, axis=1, keepdims=True)
        acc_sc[h] = alpha * acc_sc[h] + _dot(p.astype(BF16), v)
        m_sc[h] = m_new

    @pl.when(lt[step] == 1)
    def _():
        o = jnp.zeros((tq, FS_WIDTH), F32)
        for h in range(N_FS_HEADS):
            o = jnp.where(_head_lane_mask(o.shape, h), acc_sc[h] / l_sc[h], o)
        o_ref[0] = (o * _silu(g_ref[0])).astype(BF16)


def _fox_attention(q, k, v, fcol, frow, g, *, past, tq, tk):
    nb, sq, _ = q.shape
    nq = sq // tq
    tabs = _schedule(nq, lambda qi: (past + (qi + 1) * tq - 1) // tk, False)
    nsteps = int(tabs[0].shape[0])
    qoff = past // tq
    kern = functools.partial(_fox_kernel, past=past, tq=tq, tk=tk)
    return pl.pallas_call(
        kern,
        out_shape=jax.ShapeDtypeStruct((nb, sq, FS_WIDTH), BF16),
        grid_spec=pltpu.PrefetchScalarGridSpec(
            num_scalar_prefetch=4,
            grid=(nb, nsteps),
            in_specs=[
                pl.BlockSpec((1, tq, FS_WIDTH), lambda b, s, qt, kt, ft, lt: (b, qt[s], 0)),
                pl.BlockSpec((1, tk, FS_WIDTH), lambda b, s, qt, kt, ft, lt: (b, kt[s], 0)),
                pl.BlockSpec((1, tk, FS_WIDTH), lambda b, s, qt, kt, ft, lt: (b, kt[s], 0)),
                pl.BlockSpec((1, tq, N_FS_HEADS), lambda b, s, qt, kt, ft, lt: (b, qoff + qt[s], 0)),
                pl.BlockSpec((1, N_FS_HEADS, tk), lambda b, s, qt, kt, ft, lt: (b, 0, kt[s])),
                pl.BlockSpec((1, tq, FS_WIDTH), lambda b, s, qt, kt, ft, lt: (b, qt[s], 0)),
            ],
            out_specs=pl.BlockSpec((1, tq, FS_WIDTH), lambda b, s, qt, kt, ft, lt: (b, qt[s], 0)),
            scratch_shapes=[
                pltpu.VMEM((N_FS_HEADS, tq, FS_WIDTH), BF16),
                pltpu.VMEM((N_FS_HEADS, tq, 1), F32),
                pltpu.VMEM((N_FS_HEADS, tq, 1), F32),
                pltpu.VMEM((N_FS_HEADS, tq, FS_WIDTH), F32),
            ],
        ),
        compiler_params=pltpu.CompilerParams(
            dimension_semantics=("parallel", "arbitrary"), vmem_limit_bytes=VMEM_LIMIT),
        name="fox_attention",
    )(*tabs, q, k, v, fcol, frow, g)


def _sb_kernel(qt, kt, ft, lt, q_ref, k_ref, v_ref, tri_ref, g_ref, o_ref,
               qm_sc, r_sc, acc_sc, *, past, tq, tk):
    step = pl.program_id(1)
    qi = qt[step]
    kj = kt[step]

    @pl.when(ft[step] == 1)
    def _():
        q = q_ref[0]
        for h in range(N_FS_HEADS):
            qm_sc[h] = jnp.where(_head_lane_mask(q.shape, h), q, jnp.zeros_like(q))
        r_sc[...] = jnp.zeros(r_sc.shape, F32)
        acc_sc[...] = jnp.zeros(acc_sc.shape, F32)

    tri = tri_ref[...]
    q_pos = past + qi * tq + lax.broadcasted_iota(jnp.int32, (tq, 1), 0)
    for sub in reversed(range(tk // SB_SUB)):
        k = k_ref[0, sub * SB_SUB:(sub + 1) * SB_SUB, :]
        v = v_ref[0, sub * SB_SUB:(sub + 1) * SB_SUB, :]
        k_pos = kj * tk + sub * SB_SUB + lax.broadcasted_iota(jnp.int32, (1, SB_SUB), 1)
        mask = k_pos < q_pos
        for h in range(N_FS_HEADS):
            z = _dot_nt(qm_sc[h], k)
            sp = jnp.maximum(z, 0.0) + jnp.log(1.0 + jnp.exp(-jnp.abs(z)))
            l1m = jnp.where(mask, -sp, 0.0)
            hi = l1m.astype(BF16)
            lo = (l1m - hi.astype(F32)).astype(BF16)
            c = _dot(hi, tri) + _dot(lo, tri)
            r = r_sc[h]
            a = jnp.where(mask, jnp.exp(z + (c + r)), 0.0)
            acc_sc[h] = acc_sc[h] + _dot(a.astype(BF16), v)
            r_sc[h] = r + c[:, 0:1]

    @pl.when(lt[step] == 1)
    def _():
        o = jnp.zeros((tq, FS_WIDTH), F32)
        for h in range(N_FS_HEADS):
            o = jnp.where(_head_lane_mask(o.shape, h), acc_sc[h], o)
        o_ref[0] = (o * _silu(g_ref[0])).astype(BF16)


def _sb_attention(q, k, v, g, *, past, tq, tk):
    nb, sq, _ = q.shape
    nq = sq // tq
    tabs = _schedule(nq, lambda qi: (past + (qi + 1) * tq - 1) // tk, True)
    nsteps = int(tabs[0].shape[0])
    tri = jnp.asarray(np.tril(np.ones((SB_SUB, SB_SUB), np.float32))).astype(BF16)
    kern = functools.partial(_sb_kernel, past=past, tq=tq, tk=tk)
    return pl.pallas_call(
        kern,
        out_shape=jax.ShapeDtypeStruct((nb, sq, FS_WIDTH), BF16),
        grid_spec=pltpu.PrefetchScalarGridSpec(
            num_scalar_prefetch=4,
            grid=(nb, nsteps),
            in_specs=[
                pl.BlockSpec((1, tq, FS_WIDTH), lambda b, s, qt, kt, ft, lt: (b, qt[s], 0)),
                pl.BlockSpec((1, tk, FS_WIDTH), lambda b, s, qt, kt, ft, lt: (b, kt[s], 0)),
                pl.BlockSpec((1, tk, FS_WIDTH), lambda b, s, qt, kt, ft, lt: (b, kt[s], 0)),
                pl.BlockSpec((SB_SUB, SB_SUB), lambda b, s, qt, kt, ft, lt: (0, 0)),
                pl.BlockSpec((1, tq, FS_WIDTH), lambda b, s, qt, kt, ft, lt: (b, qt[s], 0)),
            ],
            out_specs=pl.BlockSpec((1, tq, FS_WIDTH), lambda b, s, qt, kt, ft, lt: (b, qt[s], 0)),
            scratch_shapes=[
                pltpu.VMEM((N_FS_HEADS, tq, FS_WIDTH), BF16),
                pltpu.VMEM((N_FS_HEADS, tq, 1), F32),
                pltpu.VMEM((N_FS_HEADS, tq, FS_WIDTH), F32),
            ],
        ),
        compiler_params=pltpu.CompilerParams(
            dimension_semantics=("parallel", "arbitrary"), vmem_limit_bytes=VMEM_LIMIT),
        name="sb_attention",
    )(*tabs, q, k, v, tri, g)


def _mla_kernel(qt, kt, ft, lt, q_ref, k_ref, wv_ref, g_ref, o_ref,
                m_sc, l_sc, acc_sc, *, past, n_kv, tq, tk):
    step = pl.program_id(1)
    qi = qt[step]
    kj = kt[step]

    @pl.when(ft[step] == 1)
    def _():
        m_sc[...] = jnp.full(m_sc.shape, NEG, F32)
        l_sc[...] = jnp.zeros(l_sc.shape, F32)
        acc_sc[...] = jnp.zeros(acc_sc.shape, F32)

    k = k_ref[0]
    v = k[:, 0:KV_LORA]
    q_pos = past + qi * tq + lax.broadcasted_iota(jnp.int32, (tq, 1), 0)
    k_pos = kj * tk + lax.broadcasted_iota(jnp.int32, (1, tk), 1)
    mask = ((k_pos >> CHUNK_SHIFT) <= (q_pos >> CHUNK_SHIFT)) & (k_pos < n_kv)
    for h in range(MLA_HEADS):
        s = _dot_nt(q_ref[0, h], k) * MLA_SCALE
        s = jnp.where(mask, s, NEG)
        m_prev = m_sc[h]
        m_new = jnp.maximum(m_prev, jnp.max(s, axis=1, keepdims=True))
        p = jnp.exp(s - m_new)
        alpha = jnp.exp(m_prev - m_new)
        l_sc[h] = alpha * l_sc[h] + jnp.sum(p, axis=1, keepdims=True)
        acc_sc[h] = alpha * acc_sc[h] + _dot(p.astype(BF16), v)
        m_sc[h] = m_new

    @pl.when(lt[step] == 1)
    def _():
        o = jnp.zeros((tq, MLA_WIDTH), F32)
        for h in range(MLA_HEADS):
            o = o + _dot((acc_sc[h] / l_sc[h]).astype(BF16), wv_ref[h])
        o_ref[0] = (o * _silu(g_ref[0])).astype(BF16)


def _mla_attention(qcat, kcat, wv, g, *, past, n_kv, tq, tk):
    nb, _, sq, _ = qcat.shape
    nq = sq // tq

    def kmax(qi):
        last_q = past + (qi + 1) * tq - 1
        end = min(((last_q >> CHUNK_SHIFT) + 1) << CHUNK_SHIFT, n_kv)
        return (end - 1) // tk

    tabs = _schedule(nq, kmax, False)
    nsteps = int(tabs[0].shape[0])
    kern = functools.partial(_mla_kernel, past=past, n_kv=n_kv, tq=tq, tk=tk)
    return pl.pallas_call(
        kern,
        out_shape=jax.ShapeDtypeStruct((nb, sq, MLA_WIDTH), BF16),
        grid_spec=pltpu.PrefetchScalarGridSpec(
            num_scalar_prefetch=4,
            grid=(nb, nsteps),
            in_specs=[
                pl.BlockSpec((1, MLA_HEADS, tq, 256), lambda b, s, qt, kt, ft, lt: (b, 0, qt[s], 0)),
                pl.BlockSpec((1, tk, 256), lambda b, s, qt, kt, ft, lt: (b, kt[s], 0)),
                pl.BlockSpec((MLA_HEADS, KV_LORA, MLA_WIDTH), lambda b, s, qt, kt, ft, lt: (0, 0, 0)),
                pl.BlockSpec((1, tq, MLA_WIDTH), lambda b, s, qt, kt, ft, lt: (b, qt[s], 0)),
            ],
            out_specs=pl.BlockSpec((1, tq, MLA_WIDTH), lambda b, s, qt, kt, ft, lt: (b, qt[s], 0)),
            scratch_shapes=[
                pltpu.VMEM((MLA_HEADS, tq, 1), F32),
                pltpu.VMEM((MLA_HEADS, tq, 1), F32),
                pltpu.VMEM((MLA_HEADS, tq, KV_LORA), F32),
            ],
        ),
        compiler_params=pltpu.CompilerParams(
            dimension_semantics=("parallel", "arbitrary"), vmem_limit_bytes=VMEM_LIMIT),
        name="mla_attention",
    )(*tabs, qcat, kcat, wv, g)


def _out_kernel(yf_ref, ys_ref, ym_ref, w_ref, x_ref, mod_ref, gpost_ref, o_ref):
    d = D_MODEL
    y = (_dot(yf_ref[0], w_ref[0:256, :]) + _dot(ys_ref[0], w_ref[256:512, :])
         + _dot(ym_ref[0], w_ref[512:1024, :]))
    gate = mod_ref[0][:, 2 * d:3 * d]
    o_ref[0] = x_ref[0] + gate * _rms(y, gpost_ref[...])


def _output(yf, ys, ym, w_out, x, mod_rows, g_post, tm):
    nb, s, d = x.shape
    mrows = mod_rows.shape[1]
    mblk = 1 if mrows == 1 else tm
    return pl.pallas_call(
        _out_kernel,
        out_shape=jax.ShapeDtypeStruct((nb, s, d), F32),
        grid=(nb, s // tm),
        in_specs=[
            pl.BlockSpec((1, tm, 256), lambda b, i: (b, i, 0)),
            pl.BlockSpec((1, tm, 256), lambda b, i: (b, i, 0)),
            pl.BlockSpec((1, tm, 512), lambda b, i: (b, i, 0)),
            pl.BlockSpec((d, d), lambda b, i: (0, 0)),
            pl.BlockSpec((1, tm, d), lambda b, i: (b, i, 0)),
            pl.BlockSpec((1, mblk, 3 * d),
                         (lambda b, i: (b, 0, 0)) if mrows == 1 else (lambda b, i: (b, i, 0))),
            pl.BlockSpec((1, d), lambda b, i: (0, 0)),
        ],
        out_specs=pl.BlockSpec((1, tm, d), lambda b, i: (b, i, 0)),
        compiler_params=pltpu.CompilerParams(
            dimension_semantics=("parallel", "arbitrary"), vmem_limit_bytes=VMEM_LIMIT),
        name="output_projection",
    )(yf, ys, ym, w_out, x, mod_rows, g_post)


def _layer_weights(l, g_pre, g_post, w_in, b_f, g_q_a, w_uq, g_kv_a, w_uk, w_uv, w_out):
    o = IN_OFFSETS
    w = w_in[l]

    def cols(i):
        return w[:, o[i]:o[i + 1]]

    kpe = cols(11)
    half = MLA_ROPE // 2
    zeros = lambda n: jnp.zeros((D_MODEL, n), F32)
    grp = jnp.concatenate([kpe, cols(3), zeros(64 - MLA_ROPE - 4),
                           -kpe[:, half:], kpe[:, :half], zeros(64 - MLA_ROPE)], axis=1)
    w_main = jnp.concatenate([cols(0), cols(1), cols(2), cols(4), cols(5), cols(6), cols(7), cols(8),
                              cols(9), cols(10), grp, cols(12)], axis=1).astype(BF16)
    bf_row = jnp.zeros((1, LANES), F32).at[0, MLA_ROPE:MLA_ROPE + 4].set(b_f[l])

    uq = w_uq[l].reshape(Q_LORA, MLA_HEADS, MLA_NOPE + MLA_ROPE)
    w_nope = uq[:, :, :MLA_NOPE].reshape(Q_LORA, MLA_HEADS * MLA_NOPE).astype(BF16)
    x1 = uq[:, :, MLA_NOPE:MLA_NOPE + half]
    x2 = uq[:, :, MLA_NOPE + half:]
    zpad = jnp.zeros((Q_LORA, MLA_HEADS, LANES - MLA_ROPE), F32)
    w_xy = jnp.concatenate([x1, x2, zpad, -x2, x1, zpad], axis=2)
    w_xy = jnp.transpose(w_xy, (1, 0, 2)).astype(BF16)

    ukt = jnp.transpose(w_uk[l], (1, 2, 0))
    zk = jnp.zeros_like(ukt)
    even = jnp.concatenate([ukt, zk], axis=1)
    odd = jnp.concatenate([zk, ukt], axis=1)
    is_odd = (jnp.arange(MLA_HEADS) % 2 == 1)[:, None, None]
    w_k = jnp.where(is_odd, odd, even).astype(BF16)

    uvt = jnp.transpose(w_uv[l], (1, 0, 2))
    sel = (jnp.arange(MLA_HEADS)[:, None] == jnp.arange(MLA_HEADS)[None, :]).astype(F32)
    w_v = (uvt[:, :, None, :] * sel[:, None, :, None]).reshape(MLA_HEADS, KV_LORA, MLA_WIDTH)
    return dict(g_pre=g_pre[l][None], g_post=g_post[l][None], w_main=w_main, bf_row=bf_row,
                g_q=g_q_a[l][None], w_nope=w_nope, w_xy=w_xy, w_k=w_k, g_kv=g_kv_a[l][None],
                w_v=w_v.astype(BF16), w_out=w_out[l].astype(BF16))


def _rope_tables(pos):
    half = MLA_ROPE // 2
    inv = ROPE_THETA ** (-jnp.arange(half, dtype=F32) / half)
    ang = pos.astype(F32)[:, None] * inv[None, :]
    pad = jnp.zeros((pos.shape[0], LANES - MLA_ROPE), F32)
    cos = jnp.concatenate([jnp.cos(ang), jnp.cos(ang), pad], axis=1)
    sin = jnp.concatenate([jnp.sin(ang), jnp.sin(ang), pad], axis=1)
    return cos, sin


def _pad_rows(a, total):
    pad = total - a.shape[1]
    if pad == 0:
        return a
    return jnp.concatenate([a, jnp.zeros((a.shape[0], pad) + a.shape[2:], a.dtype)], axis=1)


def _forget_cumsum(logf_all, skv):
    nb, n, _ = logf_all.shape
    length = -(-max(n, skv) // 1024) * 1024
    rows = jnp.transpose(_pad_rows(logf_all, length), (0, 2, 1)).reshape(nb * N_FS_HEADS, length)
    frow = _cumsum_rows(rows).reshape(nb, N_FS_HEADS, length)[:, :, :skv]
    return frow, jnp.transpose(frow, (0, 2, 1))


def _stream_layer(x, mod_rows, cos, sin, lw, past_rows, *, tm, past, fs_tiles, mla_tiles, batch):
    (fq_b, fk_f, fk_b, fv_f, fv_b, fg_f, sq_b, sk_f, sk_b, sv_f, sv_b, sg_f,
     qcat_b, ckv_f, kcat_b, mg_f, gout_f) = _projection(x, mod_rows, cos, sin, lw, tm)
    nb, rows, _ = x.shape
    seq = nb * rows // batch

    def per_batch(a):
        return a.reshape((batch, seq) + a.shape[2:])

    fq_b, fk_f, fk_b, fv_f, fv_b, fg_f, sq_b, sk_f, sk_b, sv_f, sv_b, sg_f, ckv_f, kcat_b, mg_f, gout_f = [
        per_batch(a) for a in (fq_b, fk_f, fk_b, fv_f, fv_b, fg_f, sq_b, sk_f, sk_b, sv_f, sv_b, sg_f,
                               ckv_f, kcat_b, mg_f, gout_f)]
    qcat_b = jnp.transpose(qcat_b.reshape(nb, MLA_HEADS, batch // nb, seq, 256),
                           (0, 2, 1, 3, 4)).reshape(batch, MLA_HEADS, seq, 256)
    kpe_f = gout_f[:, :, :MLA_ROPE]
    logf = gout_f[:, :, MLA_ROPE:MLA_ROPE + N_FS_HEADS]
    new = (fk_f, fv_f, logf, sk_f, sv_f, ckv_f, kpe_f)

    n_kv = past + seq
    tq_fs, tk_fs = fs_tiles
    tq_m, tk_m = mla_tiles
    skv = -(-n_kv // max(tk_fs, tk_m)) * max(tk_fs, tk_m)
    if past_rows is None:
        fk_a, fv_a, sk_a, sv_a, kc_a, logf_a = fk_b, fv_b, sk_b, sv_b, kcat_b, logf
    else:
        c_fk, c_fv, c_logf, c_sk, c_sv, c_ckv, c_kpe = past_rows

        def join(c, n):
            c = c.reshape(batch, past, -1).astype(BF16)
            return _pad_rows(jnp.concatenate([c, n], axis=1), skv)

        fk_a, fv_a, sk_a, sv_a = join(c_fk, fk_b), join(c_fv, fv_b), join(c_sk, sk_b), join(c_sv, sv_b)
        c_kc = jnp.concatenate([c_ckv, c_kpe, jnp.zeros((batch, past, LANES - MLA_ROPE), F32)], axis=2)
        kc_a = join(c_kc, kcat_b)
        logf_a = jnp.concatenate([c_logf, logf], axis=1)
    frow, fcol = _forget_cumsum(logf_a, skv)

    y_fox = _fox_attention(fq_b, fk_a, fv_a, fcol, frow, fg_f, past=past, tq=tq_fs, tk=tk_fs)
    y_sb = _sb_attention(sq_b, sk_a, sv_a, sg_f, past=past, tq=tq_fs, tk=tk_fs)
    y_mla = _mla_attention(qcat_b, kc_a, lw["w_v"], mg_f, past=past, n_kv=n_kv, tq=tq_m, tk=tk_m)

    def per_block(a):
        return a.reshape((nb, rows) + a.shape[2:])

    x_new = _output(per_block(y_fox), per_block(y_sb), per_block(y_mla), lw["w_out"], x, mod_rows,
                    lw["g_post"], tm)
    return x_new, new


def kernel(x_prompt, x_sample, c_prompt, c_sample, cache_fox_k, cache_fox_v, cache_fox_logf, cache_sb_k, cache_sb_v, cache_mla_ckv, cache_mla_kpe, g_pre, g_post, w_ada, b_ada, w_in, b_f, g_q_a, w_uq, g_kv_a, w_uk, w_uv, w_out):
    batch, seq, d = x_prompt.shape
    dec_batch, dec_seq, _ = x_sample.shape
    past_len = cache_fox_k.shape[2]
    dec_rows = dec_batch * dec_seq

    mod = _modulation(jnp.concatenate([c_prompt, c_sample], axis=0), w_ada, b_ada)
    cos_p, sin_p = _rope_tables(jnp.arange(seq, dtype=jnp.int32))
    pos_s = past_len + (jnp.arange(dec_rows, dtype=jnp.int32) % dec_seq)
    cos_s, sin_s = _rope_tables(pos_s)
    skv_s = -(-(past_len + dec_seq) // SB_SUB) * SB_SUB

    y_p = x_prompt
    y_s = x_sample.reshape(1, dec_rows, d)
    rows_p, rows_s = [], []
    for l in range(DEPTH):
        lw = _layer_weights(l, g_pre, g_post, w_in, b_f, g_q_a, w_uq, g_kv_a, w_uk, w_uv, w_out)
        mod_p = mod[l, :batch][:, None, :]
        mod_s = jnp.repeat(mod[l, batch:], dec_seq, axis=0)[None]
        y_p, new_p = _stream_layer(y_p, mod_p, cos_p, sin_p, lw, None, tm=512, past=0,
                                   fs_tiles=(256, 512), mla_tiles=(128, 512), batch=batch)
        past_rows = (cache_fox_k[l], cache_fox_v[l], cache_fox_logf[l], cache_sb_k[l], cache_sb_v[l],
                     cache_mla_ckv[l], cache_mla_kpe[l])
        y_s, new_s = _stream_layer(y_s, mod_s, cos_s, sin_s, lw, past_rows, tm=dec_rows, past=past_len,
                                   fs_tiles=(dec_seq, skv_s), mla_tiles=(dec_seq, skv_s),
                                   batch=dec_batch)
        rows_p.append(new_p)
        rows_s.append(new_s)

    def stack(rows, idx, shape_tail):
        a = jnp.stack([r[idx] for r in rows])
        return a.reshape(a.shape[:3] + shape_tail)

    heads = (N_FS_HEADS, HEAD_DIM)
    tails = (heads, heads, (N_FS_HEADS,), heads, heads, (KV_LORA,), (MLA_ROPE,))
    outs_p = [stack(rows_p, i, t) for i, t in enumerate(tails)]
    outs_s = [stack(rows_s, i, t) for i, t in enumerate(tails)]
    return (y_p, y_s.reshape(dec_batch, dec_seq, d), *outs_p, *outs_s)
```

```python
import functools

import numpy as np
import jax
import jax.numpy as jnp
from jax import lax
from jax.experimental import pallas as pl
from jax.experimental.pallas import tpu as pltpu

D_MODEL = 1024
DEPTH = 2
CHUNK_SHIFT = 6
HEAD_DIM = 64
N_FS_HEADS = 4
FS_WIDTH = N_FS_HEADS * HEAD_DIM
MLA_HEADS = 8
MLA_GROUP = 8
MLA_NOPE = 64
MLA_ROPE = 32
MLA_V = 64
MLA_WIDTH = MLA_HEADS * MLA_V
Q_LORA = 256
KV_LORA = 128
ROPE_THETA = 10000.0
EPS = 1e-6
IN_SPLITS = (256, 256, 256, 4, 256, 256, 256, 256, 256, Q_LORA, KV_LORA, MLA_ROPE, MLA_WIDTH)
IN_OFFSETS = tuple(int(v) for v in np.cumsum((0,) + IN_SPLITS))

LANES = 128
SB_SUB = 256
W_MAIN = 3072
NEG = -1e30
SB_DEAD = -110.0
FS_SCALE = HEAD_DIM ** -0.5
MLA_SCALE = (MLA_NOPE + MLA_ROPE) ** -0.5
MLA_EXP2_SCALE = MLA_SCALE * float(np.log2(np.e))
ONES_LANE = KV_LORA + MLA_ROPE
VMEM_LIMIT = 56 * 1024 * 1024

BF16 = jnp.bfloat16
F32 = jnp.float32


def _dot(a, b):
    return jnp.dot(a, b, preferred_element_type=F32)


def _dot_nt(a, b):
    return lax.dot_general(a, b, (((1,), (1,)), ((), ())), preferred_element_type=F32)


def _silu(g):
    return g / (1.0 + jnp.exp(-g))


def _rms(x, g):
    return x * lax.rsqrt(jnp.mean(x * x, axis=-1, keepdims=True) + EPS) * g


def _mod_kernel(c_ref, w_ref, b_ref, o_ref):
    a = _silu(c_ref[...]).astype(BF16)
    o_ref[0] = _dot(a, w_ref[0].astype(BF16)) + b_ref[0]


def _modulation(c_all, w_ada, b_ada):
    n = c_all.shape[0]
    d = D_MODEL
    return pl.pallas_call(
        _mod_kernel,
        out_shape=jax.ShapeDtypeStruct((DEPTH, n, 3 * d), F32),
        grid=(DEPTH, 3),
        in_specs=[
            pl.BlockSpec((n, d), lambda l, j: (0, 0)),
            pl.BlockSpec((1, d, d), lambda l, j: (l, 0, j)),
            pl.BlockSpec((1, 1, d), lambda l, j: (l, 0, j)),
        ],
        out_specs=pl.BlockSpec((1, n, d), lambda l, j: (l, 0, j)),
        compiler_params=pltpu.CompilerParams(
            dimension_semantics=("arbitrary", "arbitrary"), vmem_limit_bytes=VMEM_LIMIT),
        name="modulation",
    )(c_all, w_ada, b_ada.reshape(DEPTH, 1, 3 * d))


def _proj_kernel(x_ref, mod_ref, gpre_ref, w_ref, bf_ref, gq_ref, wn_ref, wxy_ref, wk_ref,
                 gkv_ref, cos_ref, sin_ref,
                 fq_b, fk_f, fk_b, fv_f, fv_b, fg_f, sq_b, sk_f, sk_b, sv_f, sv_b, sg_f,
                 qcat_b, ckv_f, kcat_b, mg_f, gout_f, fv_t, sv_t, ckv_t):
    d = D_MODEL
    x = x_ref[0]
    mod = mod_ref[0]
    shift = mod[:, 0:d]
    scale = mod[:, d:2 * d]
    h = _rms(x, gpre_ref[...]) * (1.0 + scale) + shift
    hb = h.astype(BF16)

    def proj(a, b):
        return _dot(hb, w_ref[:, a:b])

    fq_b[0] = (proj(0, 256) * FS_SCALE).astype(BF16)
    t = proj(256, 512)
    fk_f[0] = t
    fk_b[0] = t.astype(BF16)
    t = proj(512, 768)
    fv_f[0] = t
    fv_b[0] = t.astype(BF16)
    fv_t[0] = t.T.astype(BF16)
    fg_f[0] = proj(768, 1024)
    sq_b[0] = (proj(1024, 1280) * FS_SCALE).astype(BF16)
    t = proj(1280, 1536)
    sk_f[0] = t
    sk_b[0] = t.astype(BF16)
    t = proj(1536, 1792)
    sv_f[0] = t
    sv_b[0] = t.astype(BF16)
    sv_t[0] = t.T.astype(BF16)
    sg_f[0] = proj(1792, 2048)
    mg_f[0] = proj(2560, 3072)

    cos = cos_ref[...]
    sin = sin_ref[...]

    cg = proj(2304, 2560)
    ckvn = _rms(cg[:, 0:KV_LORA], gkv_ref[...])
    ckv_f[0] = ckvn
    ckv_t[0] = ckvn.T.astype(BF16)
    grp = cg[:, KV_LORA:2 * KV_LORA]
    rope_k = grp * cos + pltpu.roll(grp, 64, axis=1) * sin
    zf = grp + bf_ref[...]
    logf = jnp.minimum(zf, 0.0) - jnp.log(1.0 + jnp.exp(-jnp.abs(zf)))
    lane = lax.broadcasted_iota(jnp.int32, grp.shape, 1)
    gout_f[0] = jnp.where(lane < MLA_ROPE, rope_k, logf)
    ones_lane = jnp.where(lane == ONES_LANE - KV_LORA, 1.0, 0.0)
    kcat_b[0] = jnp.concatenate([ckvn, rope_k + ones_lane], axis=1).astype(BF16)

    cqn = _rms(proj(2048, 2304), gq_ref[...]).astype(BF16)
    qn = _dot(cqn, wn_ref[...]).astype(BF16)
    for hh in range(MLA_HEADS):
        pair = qn[:, (hh // 2) * LANES:(hh // 2 + 1) * LANES]
        qlat = _dot(pair, wk_ref[hh])
        xy = _dot(cqn, wxy_ref[hh])
        rope_q = xy[:, 0:LANES] * cos + xy[:, LANES:2 * LANES] * sin
        qcat_b[0, hh] = jnp.concatenate([qlat, rope_q], axis=1).astype(BF16)


def _projection(x, mod_rows, cos, sin, lw, tm):
    nb, s, d = x.shape
    mrows = mod_rows.shape[1]
    mblk = 1 if mrows == 1 else tm
    grid = (nb, s // tm)

    def row(width, dtype):
        return (jax.ShapeDtypeStruct((nb, s, width), dtype),
                pl.BlockSpec((1, tm, width), lambda b, i: (b, i, 0)))

    def col(width):
        return (jax.ShapeDtypeStruct((nb, width, s), BF16),
                pl.BlockSpec((1, width, tm), lambda b, i: (b, 0, i)))

    outs = [row(256, BF16), row(256, F32), row(256, BF16), row(256, F32), row(256, BF16),
            row(256, F32), row(256, BF16), row(256, F32), row(256, BF16), row(256, F32),
            row(256, BF16), row(256, F32),
            (jax.ShapeDtypeStruct((nb, MLA_HEADS, s, 256), BF16),
             pl.BlockSpec((1, MLA_HEADS, tm, 256), lambda b, i: (b, 0, i, 0))),
            row(KV_LORA, F32), row(256, BF16), row(MLA_WIDTH, F32), row(LANES, F32),
            col(256), col(256), col(KV_LORA)]

    def full(a):
        nd = a.ndim
        return pl.BlockSpec(a.shape, lambda b, i: (0,) * nd)

    consts = [lw["g_pre"], lw["w_main"], lw["bf_row"], lw["g_q"], lw["w_nope"], lw["w_xy"],
              lw["w_k"], lw["g_kv"]]
    in_specs = ([pl.BlockSpec((1, tm, d), lambda b, i: (b, i, 0)),
                 pl.BlockSpec((1, mblk, 3 * d),
                              (lambda b, i: (b, 0, 0)) if mrows == 1 else (lambda b, i: (b, i, 0)))]
                + [full(a) for a in consts]
                + [pl.BlockSpec((tm, LANES), lambda b, i: (i, 0)),
                   pl.BlockSpec((tm, LANES), lambda b, i: (i, 0))])
    return pl.pallas_call(
        _proj_kernel,
        out_shape=[o[0] for o in outs],
        grid=grid,
        in_specs=in_specs,
        out_specs=[o[1] for o in outs],
        compiler_params=pltpu.CompilerParams(
            dimension_semantics=("parallel", "arbitrary"), vmem_limit_bytes=VMEM_LIMIT),
        name="projection",
    )(x, mod_rows, *consts, cos, sin)


def _cumsum_kernel(x_ref, u_ref, l_ref, o_ref):
    x = x_ref[0]
    w = jnp.dot(x, u_ref[...], preferred_element_type=F32, precision=lax.Precision.HIGHEST)
    tot = jnp.broadcast_to(w[:, LANES - 1:LANES], w.shape)
    off = jnp.dot(l_ref[...], tot, preferred_element_type=F32, precision=lax.Precision.HIGHEST)
    o_ref[0] = w + off


def _cumsum_rows(x):
    r, length = x.shape
    n = length // LANES
    u = jnp.asarray(np.triu(np.ones((LANES, LANES), np.float32)))
    lo = jnp.asarray(np.tril(np.ones((n, n), np.float32), -1))
    out = pl.pallas_call(
        _cumsum_kernel,
        out_shape=jax.ShapeDtypeStruct((r, n, LANES), F32),
        grid=(r,),
        in_specs=[pl.BlockSpec((1, n, LANES), lambda i: (i, 0, 0)),
                  pl.BlockSpec((LANES, LANES), lambda i: (0, 0)),
                  pl.BlockSpec((n, n), lambda i: (0, 0))],
        out_specs=pl.BlockSpec((1, n, LANES), lambda i: (i, 0, 0)),
        compiler_params=pltpu.CompilerParams(dimension_semantics=("arbitrary",)),
        name="cumsum_logf",
    )(x.reshape(r, n, LANES), u, lo)
    return out.reshape(r, length)


def _schedule(nq, kmax_fn, reverse):
    qi_l, kj_l, first_l, last_l = [], [], [], []
    for qi in range(nq):
        ks = list(range(kmax_fn(qi) + 1))
        if reverse:
            ks = ks[::-1]
        for n, kj in enumerate(ks):
            qi_l.append(qi)
            kj_l.append(kj)
            first_l.append(int(n == 0))
            last_l.append(int(n == len(ks) - 1))
    return tuple(jnp.asarray(np.asarray(a, np.int32)) for a in (qi_l, kj_l, first_l, last_l))


def _head_lane_mask(shape, h):
    lane = lax.broadcasted_iota(jnp.int32, shape, 1)
    return (lane >= HEAD_DIM * h) & (lane < HEAD_DIM * (h + 1))


def _fox_kernel(qt, kt, ft, lt, q_ref, k_ref, v_ref, fc_ref, fr_ref, g_ref, o_ref,
                qm_sc, m_sc, l_sc, acc_sc, *, past, tq, tk):
    step = pl.program_id(1)
    qi = qt[step]
    kj = kt[step]

    @pl.when(ft[step] == 1)
    def _():
        q = q_ref[0]
        for h in range(N_FS_HEADS):
            qm_sc[h] = jnp.where(_head_lane_mask(q.shape, h), q, jnp.zeros_like(q))
        m_sc[...] = jnp.full(m_sc.shape, NEG, F32)
        l_sc[...] = jnp.zeros(l_sc.shape, F32)
        acc_sc[...] = jnp.zeros(acc_sc.shape, F32)

    k = k_ref[0]
    v = v_ref[0]
    q_pos = past + qi * tq + lax.broadcasted_iota(jnp.int32, (tq, 1), 0)
    k_pos = kj * tk + lax.broadcasted_iota(jnp.int32, (1, tk), 1)
    mask = k_pos <= q_pos
    fc = fc_ref[0]
    fr = fr_ref[0]
    for h in range(N_FS_HEADS):
        s = _dot_nt(qm_sc[h], k)
        s = s + (fc[:, h:h + 1] - fr[h:h + 1, :])
        s = jnp.where(mask, s, NEG)
        m_prev = m_sc[h]
        m_new = jnp.maximum(m_prev, jnp.max(s, axis=1, keepdims=True))
        p = jnp.exp(s - m_new)
        alpha = jnp.exp(m_prev - m_new)
        l_sc[h] = alpha * l_sc[h] + jnp.sum(p, axis=1, keepdims=True)
        acc_sc[h] = alpha * acc_sc[h] + _dot(p.astype(BF16), v)
        m_sc[h] = m_new

    @pl.when(lt[step] == 1)
    def _():
        o = jnp.zeros((tq, FS_WIDTH), F32)
        for h in range(N_FS_HEADS):
            o = jnp.where(_head_lane_mask(o.shape, h), acc_sc[h] / l_sc[h], o)
        o_ref[0] = (o * _silu(g_ref[0])).astype(BF16)


def _fox_attention(q, k, v, fcol, frow, g, *, past, tq, tk):
    nb, sq, _ = q.shape
    nq = sq // tq
    tabs = _schedule(nq, lambda qi: (past + (qi + 1) * tq - 1) // tk, False)
    nsteps = int(tabs[0].shape[0])
    qoff = past // tq
    kern = functools.partial(_fox_kernel, past=past, tq=tq, tk=tk)
    return pl.pallas_call(
        kern,
        out_shape=jax.ShapeDtypeStruct((nb, sq, FS_WIDTH), BF16),
        grid_spec=pltpu.PrefetchScalarGridSpec(
            num_scalar_prefetch=4,
            grid=(nb, nsteps),
            in_specs=[
                pl.BlockSpec((1, tq, FS_WIDTH), lambda b, s, qt, kt, ft, lt: (b, qt[s], 0)),
                pl.BlockSpec((1, tk, FS_WIDTH), lambda b, s, qt, kt, ft, lt: (b, kt[s], 0)),
                pl.BlockSpec((1, tk, FS_WIDTH), lambda b, s, qt, kt, ft, lt: (b, kt[s], 0)),
                pl.BlockSpec((1, tq, N_FS_HEADS), lambda b, s, qt, kt, ft, lt: (b, qoff + qt[s], 0)),
                pl.BlockSpec((1, N_FS_HEADS, tk), lambda b, s, qt, kt, ft, lt: (b, 0, kt[s])),
                pl.BlockSpec((1, tq, FS_WIDTH), lambda b, s, qt, kt, ft, lt: (b, qt[s], 0)),
            ],
            out_specs=pl.BlockSpec((1, tq, FS_WIDTH), lambda b, s, qt, kt, ft, lt: (b, qt[s], 0)),
            scratch_shapes=[
                pltpu.VMEM((N_FS_HEADS, tq, FS_WIDTH), BF16),
                pltpu.VMEM((N_FS_HEADS, tq, 1), F32),
                pltpu.VMEM((N_FS_HEADS, tq, 1), F32),
                pltpu.VMEM((N_FS_HEADS, tq, FS_WIDTH), F32),
            ],
        ),
        compiler_params=pltpu.CompilerParams(
            dimension_semantics=("parallel", "arbitrary"), vmem_limit_bytes=VMEM_LIMIT),
        name="fox_attention",
    )(*tabs, q, k, v, fcol, frow, g)


def _sb_kernel(qt, kt, ft, lt, q_ref, k_ref, v_ref, tri_ref, g_ref, o_ref,
               qm_sc, r_sc, acc_sc, *, past, tq, tk):
    step = pl.program_id(1)
    qi = qt[step]
    kj = kt[step]

    @pl.when(ft[step] == 1)
    def _():
        q = q_ref[0]
        for h in range(N_FS_HEADS):
            qm_sc[h] = jnp.where(_head_lane_mask(q.shape, h), q, jnp.zeros_like(q))
        r_sc[...] = jnp.zeros(r_sc.shape, F32)
        acc_sc[...] = jnp.zeros(acc_sc.shape, F32)

    tri = tri_ref[...]
    q_pos = past + qi * tq + lax.broadcasted_iota(jnp.int32, (tq, 1), 0)
    for sub in reversed(range(tk // SB_SUB)):
        k = k_ref[0, sub * SB_SUB:(sub + 1) * SB_SUB, :]
        v = v_ref[0, sub * SB_SUB:(sub + 1) * SB_SUB, :]
        k_pos = kj * tk + sub * SB_SUB + lax.broadcasted_iota(jnp.int32, (1, SB_SUB), 1)
        mask = k_pos < q_pos
        for h in range(N_FS_HEADS):
            z = _dot_nt(qm_sc[h], k)
            sp = jnp.maximum(z, 0.0) + jnp.log(1.0 + jnp.exp(-jnp.abs(z)))
            l1m = jnp.where(mask, -sp, 0.0)
            hi = l1m.astype(BF16)
            lo = (l1m - hi.astype(F32)).astype(BF16)
            c = _dot(hi, tri) + _dot(lo, tri)
            r = r_sc[h]
            a = jnp.where(mask, jnp.exp(z + (c + r)), 0.0)
            acc_sc[h] = acc_sc[h] + _dot(a.astype(BF16), v)
            r_sc[h] = r + c[:, 0:1]

    @pl.when(lt[step] == 1)
    def _():
        o = jnp.zeros((tq, FS_WIDTH), F32)
        for h in range(N_FS_HEADS):
            o = jnp.where(_head_lane_mask(o.shape, h), acc_sc[h], o)
        o_ref[0] = (o * _silu(g_ref[0])).astype(BF16)


def _sb_attention(q, k, v, g, *, past, tq, tk):
    nb, sq, _ = q.shape
    nq = sq // tq
    tabs = _schedule(nq, lambda qi: (past + (qi + 1) * tq - 1) // tk, True)
    nsteps = int(tabs[0].shape[0])
    tri = jnp.asarray(np.tril(np.ones((SB_SUB, SB_SUB), np.float32))).astype(BF16)
    kern = functools.partial(_sb_kernel, past=past, tq=tq, tk=tk)
    return pl.pallas_call(
        kern,
        out_shape=jax.ShapeDtypeStruct((nb, sq, FS_WIDTH), BF16),
        grid_spec=pltpu.PrefetchScalarGridSpec(
            num_scalar_prefetch=4,
            grid=(nb, nsteps),
            in_specs=[
                pl.BlockSpec((1, tq, FS_WIDTH), lambda b, s, qt, kt, ft, lt: (b, qt[s], 0)),
                pl.BlockSpec((1, tk, FS_WIDTH), lambda b, s, qt, kt, ft, lt: (b, kt[s], 0)),
                pl.BlockSpec((1, tk, FS_WIDTH), lambda b, s, qt, kt, ft, lt: (b, kt[s], 0)),
                pl.BlockSpec((SB_SUB, SB_SUB), lambda b, s, qt, kt, ft, lt: (0, 0)),
                pl.BlockSpec((1, tq, FS_WIDTH), lambda b, s, qt, kt, ft, lt: (b, qt[s], 0)),
            ],
            out_specs=pl.BlockSpec((1, tq, FS_WIDTH), lambda b, s, qt, kt, ft, lt: (b, qt[s], 0)),
            scratch_shapes=[
                pltpu.VMEM((N_FS_HEADS, tq, FS_WIDTH), BF16),
                pltpu.VMEM((N_FS_HEADS, tq, 1), F32),
                pltpu.VMEM((N_FS_HEADS, tq, FS_WIDTH), F32),
            ],
        ),
        compiler_params=pltpu.CompilerParams(
            dimension_semantics=("parallel", "arbitrary"), vmem_limit_bytes=VMEM_LIMIT),
        name="sb_attention",
    )(*tabs, q, k, v, tri, g)


def _mla_kernel(qt, kt, ft, lt, q_ref, k_ref, wv_ref, g_ref, o_ref,
                m_sc, acc_sc, *, past, n_kv, tq, tk):
    step = pl.program_id(1)
    qi = qt[step]
    kj = kt[step]
    rows = MLA_HEADS * tq

    @pl.when(ft[step] == 1)
    def _():
        m_sc[...] = jnp.full(m_sc.shape, NEG, F32)
        acc_sc[...] = jnp.zeros(acc_sc.shape, F32)

    k = k_ref[0]
    q_pos = past + qi * tq + lax.broadcasted_iota(jnp.int32, (tq, 1), 0)
    k_pos = kj * tk + lax.broadcasted_iota(jnp.int32, (1, tk), 1)
    mask = ((k_pos >> CHUNK_SHIFT) <= (q_pos >> CHUNK_SHIFT)) & (k_pos < n_kv)
    s = _dot_nt(q_ref[0].reshape(rows, 256), k).reshape(MLA_HEADS, tq, tk)
    s = jnp.where(mask[None], s, NEG)
    m_prev = m_sc[...]
    m_new = jnp.maximum(m_prev, jnp.max(s, axis=2, keepdims=True))
    p = jnp.exp2((s - m_new) * MLA_EXP2_SCALE)
    alpha = jnp.exp2((m_prev - m_new) * MLA_EXP2_SCALE)
    pv = _dot(p.reshape(rows, tk).astype(BF16), k)
    acc_sc[...] = alpha.reshape(rows, 1) * acc_sc[...] + pv
    m_sc[...] = m_new

    @pl.when(lt[step] == 1)
    def _():
        acc = acc_sc[...]
        o_lat = (acc[:, 0:KV_LORA] / acc[:, ONES_LANE:ONES_LANE + 1]).astype(BF16)
        o = jnp.zeros((tq, MLA_WIDTH), F32)
        for h in range(MLA_HEADS):
            o = o + _dot(o_lat[h * tq:(h + 1) * tq], wv_ref[h])
        o_ref[0] = (o * _silu(g_ref[0])).astype(BF16)


def _mla_attention(qcat, kcat, wv, g, *, past, n_kv, tq, tk):
    nb, _, sq, _ = qcat.shape
    nq = sq // tq

    def kmax(qi):
        last_q = past + (qi + 1) * tq - 1
        end = min(((last_q >> CHUNK_SHIFT) + 1) << CHUNK_SHIFT, n_kv)
        return (end - 1) // tk

    tabs = _schedule(nq, kmax, False)
    nsteps = int(tabs[0].shape[0])
    kern = functools.partial(_mla_kernel, past=past, n_kv=n_kv, tq=tq, tk=tk)
    return pl.pallas_call(
        kern,
        out_shape=jax.ShapeDtypeStruct((nb, sq, MLA_WIDTH), BF16),
        grid_spec=pltpu.PrefetchScalarGridSpec(
            num_scalar_prefetch=4,
            grid=(nb, nsteps),
            in_specs=[
                pl.BlockSpec((1, MLA_HEADS, tq, 256), lambda b, s, qt, kt, ft, lt: (b, 0, qt[s], 0)),
                pl.BlockSpec((1, tk, 256), lambda b, s, qt, kt, ft, lt: (b, kt[s], 0)),
                pl.BlockSpec((MLA_HEADS, KV_LORA, MLA_WIDTH), lambda b, s, qt, kt, ft, lt: (0, 0, 0)),
                pl.BlockSpec((1, tq, MLA_WIDTH), lambda b, s, qt, kt, ft, lt: (b, qt[s], 0)),
            ],
            out_specs=pl.BlockSpec((1, tq, MLA_WIDTH), lambda b, s, qt, kt, ft, lt: (b, qt[s], 0)),
            scratch_shapes=[
                pltpu.VMEM((MLA_HEADS, tq, 1), F32),
                pltpu.VMEM((MLA_HEADS * tq, 256), F32),
            ],
        ),
        compiler_params=pltpu.CompilerParams(
            dimension_semantics=("parallel", "arbitrary"), vmem_limit_bytes=VMEM_LIMIT),
        name="mla_attention",
    )(*tabs, qcat, kcat, wv, g)


def _stack_heads(q, qm_sc, tq):
    for h in range(N_FS_HEADS):
        qm_sc[h * tq:(h + 1) * tq, :] = jnp.where(_head_lane_mask(q.shape, h), q, jnp.zeros_like(q))


def _query_positions(past, qi, tq, heads):
    lane = lax.broadcasted_iota(jnp.int32, (1, heads * tq), 1)
    return past + qi * tq + (lane & (tq - 1))


def _fox_t_kernel(qt, kt, ft, lt, q_ref, k_ref, vt_ref, fc_ref, fr_ref, g_ref, o_ref,
                  qm_sc, fkb_sc, fq_sc, m_sc, l_sc, acc_sc, *, past, tq, tk):
    step = pl.program_id(1)
    qi = qt[step]
    kj = kt[step]
    seq = fkb_sc.shape[1]

    @pl.when(step == 0)
    def _():
        fc = fc_ref[0]
        for h in range(N_FS_HEADS):
            fkb_sc[h] = jnp.broadcast_to(fc[:, h:h + 1], (seq, LANES))

    @pl.when(ft[step] == 1)
    def _():
        _stack_heads(q_ref[0], qm_sc, tq)
        fr = fr_ref[0]
        fq_sc[...] = jnp.concatenate([fr[h:h + 1, :] for h in range(N_FS_HEADS)], axis=1)
        m_sc[...] = jnp.full(m_sc.shape, NEG, F32)
        l_sc[...] = jnp.zeros(l_sc.shape, F32)
        acc_sc[...] = jnp.zeros(acc_sc.shape, F32)

    def body(masked):
        s_t = _dot_nt(k_ref[0], qm_sc[...])
        fk = fkb_sc[:, pl.ds(pl.multiple_of(kj * tk, tk), tk), :]
        groups = tq // LANES
        s_t = jnp.concatenate(
            [s_t[:, g * LANES:(g + 1) * LANES] - fk[g // groups] for g in range(N_FS_HEADS * groups)],
            axis=1) + fq_sc[...]
        if masked:
            k_pos = kj * tk + lax.broadcasted_iota(jnp.int32, (tk, 1), 0)
            s_t = jnp.where(k_pos <= _query_positions(past, qi, tq, N_FS_HEADS), s_t, NEG)
        m_prev = m_sc[...]
        m_new = jnp.maximum(m_prev, jnp.max(s_t, axis=0, keepdims=True))
        p_t = jnp.exp(s_t - m_new)
        alpha = jnp.exp(m_prev - m_new)
        l_sc[...] = alpha * l_sc[...] + jnp.sum(p_t, axis=0, keepdims=True)
        m_sc[...] = m_new
        p_b = p_t.astype(BF16)
        for h in range(N_FS_HEADS):
            pv = _dot(vt_ref[0, h * HEAD_DIM:(h + 1) * HEAD_DIM, :], p_b[:, h * tq:(h + 1) * tq])
            acc_sc[h] = alpha[:, h * tq:(h + 1) * tq] * acc_sc[h] + pv

    @pl.when(lt[step] == 0)
    def _():
        body(False)

    @pl.when(lt[step] == 1)
    def _():
        body(True)
        l = l_sc[...]
        o_t = jnp.concatenate([acc_sc[h] / l[:, h * tq:(h + 1) * tq] for h in range(N_FS_HEADS)], axis=0)
        o_ref[0] = (o_t.T * _silu(g_ref[0])).astype(BF16)


def _fox_attention_t(q, k, vt, fcol, frow, g, *, tq, tk):
    nb, sq, _ = q.shape
    nq = sq // tq
    tabs = _schedule(nq, lambda qi: ((qi + 1) * tq - 1) // tk, False)
    nsteps = int(tabs[0].shape[0])
    rows = N_FS_HEADS * tq
    kern = functools.partial(_fox_t_kernel, past=0, tq=tq, tk=tk)
    return pl.pallas_call(
        kern,
        out_shape=jax.ShapeDtypeStruct((nb, sq, FS_WIDTH), BF16),
        grid_spec=pltpu.PrefetchScalarGridSpec(
            num_scalar_prefetch=4,
            grid=(nb, nsteps),
            in_specs=[
                pl.BlockSpec((1, tq, FS_WIDTH), lambda b, s, qt, kt, ft, lt: (b, qt[s], 0)),
                pl.BlockSpec((1, tk, FS_WIDTH), lambda b, s, qt, kt, ft, lt: (b, kt[s], 0)),
                pl.BlockSpec((1, FS_WIDTH, tk), lambda b, s, qt, kt, ft, lt: (b, 0, kt[s])),
                pl.BlockSpec((1, sq, N_FS_HEADS), lambda b, s, qt, kt, ft, lt: (b, 0, 0)),
                pl.BlockSpec((1, N_FS_HEADS, tq), lambda b, s, qt, kt, ft, lt: (b, 0, qt[s])),
                pl.BlockSpec((1, tq, FS_WIDTH), lambda b, s, qt, kt, ft, lt: (b, qt[s], 0)),
            ],
            out_specs=pl.BlockSpec((1, tq, FS_WIDTH), lambda b, s, qt, kt, ft, lt: (b, qt[s], 0)),
            scratch_shapes=[
                pltpu.VMEM((rows, FS_WIDTH), BF16),
                pltpu.VMEM((N_FS_HEADS, sq, LANES), F32),
                pltpu.VMEM((1, rows), F32),
                pltpu.VMEM((1, rows), F32),
                pltpu.VMEM((1, rows), F32),
                pltpu.VMEM((N_FS_HEADS, HEAD_DIM, tq), F32),
            ],
        ),
        compiler_params=pltpu.CompilerParams(
            dimension_semantics=("parallel", "arbitrary"), vmem_limit_bytes=VMEM_LIMIT),
        name="fox_attention_t",
    )(*tabs, q, k, vt, fcol, frow, g)


def _sb_t_kernel(qt, kt, ft, lt, q_ref, k_ref, vt_ref, tri_ref, g_ref, o_ref,
                 qm_sc, r_sc, acc_sc, *, past, tq, tk):
    step = pl.program_id(1)
    qi = qt[step]
    kj = kt[step]

    @pl.when(ft[step] == 1)
    def _():
        _stack_heads(q_ref[0], qm_sc, tq)
        r_sc[...] = jnp.zeros(r_sc.shape, F32)
        acc_sc[...] = jnp.zeros(acc_sc.shape, F32)

    def body(masked):
        z_all = _dot_nt(k_ref[0], qm_sc[...])
        tri_t = tri_ref[...]
        for sub in reversed(range(tk // SB_SUB)):
            z = z_all[sub * SB_SUB:(sub + 1) * SB_SUB, :]
            l1m = -(jnp.maximum(z, 0.0) + jnp.log(1.0 + jnp.exp(-jnp.abs(z))))
            if masked:
                k_pos = kj * tk + sub * SB_SUB + lax.broadcasted_iota(jnp.int32, (SB_SUB, 1), 0)
                mask = k_pos < _query_positions(past, qi, tq, N_FS_HEADS)
                l1m = jnp.where(mask, l1m, 0.0)
            hi = l1m.astype(BF16)
            lo = (l1m - hi.astype(F32)).astype(BF16)
            c = _dot(tri_t, hi) + _dot(tri_t, lo)
            r = r_sc[...]
            a = jnp.exp(z + (c + r))
            if masked:
                a = jnp.where(mask, a, 0.0)
            a_b = a.astype(BF16)
            for h in range(N_FS_HEADS):
                acc_sc[h] = acc_sc[h] + _dot(
                    vt_ref[0, h * HEAD_DIM:(h + 1) * HEAD_DIM, sub * SB_SUB:(sub + 1) * SB_SUB],
                    a_b[:, h * tq:(h + 1) * tq])
            r_sc[...] = r + c[0:1, :]

    @pl.when(ft[step] == 1)
    def _():
        body(True)

    live = jnp.max(r_sc[...]) > SB_DEAD

    @pl.when(jnp.logical_and(ft[step] == 0, live))
    def _():
        body(False)

    @pl.when(lt[step] == 1)
    def _():
        o_t = jnp.concatenate([acc_sc[h] for h in range(N_FS_HEADS)], axis=0)
        o_ref[0] = (o_t.T * _silu(g_ref[0])).astype(BF16)


def _sb_attention_t(q, k, vt, g, *, tq, tk):
    nb, sq, _ = q.shape
    nq = sq // tq
    tabs = _schedule(nq, lambda qi: ((qi + 1) * tq - 1) // tk, True)
    nsteps = int(tabs[0].shape[0])
    rows = N_FS_HEADS * tq
    tri_t = jnp.asarray(np.triu(np.ones((SB_SUB, SB_SUB), np.float32))).astype(BF16)
    kern = functools.partial(_sb_t_kernel, past=0, tq=tq, tk=tk)
    return pl.pallas_call(
        kern,
        out_shape=jax.ShapeDtypeStruct((nb, sq, FS_WIDTH), BF16),
        grid_spec=pltpu.PrefetchScalarGridSpec(
            num_scalar_prefetch=4,
            grid=(nb, nsteps),
            in_specs=[
                pl.BlockSpec((1, tq, FS_WIDTH), lambda b, s, qt, kt, ft, lt: (b, qt[s], 0)),
                pl.BlockSpec((1, tk, FS_WIDTH), lambda b, s, qt, kt, ft, lt: (b, kt[s], 0)),
                pl.BlockSpec((1, FS_WIDTH, tk), lambda b, s, qt, kt, ft, lt: (b, 0, kt[s])),
                pl.BlockSpec((SB_SUB, SB_SUB), lambda b, s, qt, kt, ft, lt: (0, 0)),
                pl.BlockSpec((1, tq, FS_WIDTH), lambda b, s, qt, kt, ft, lt: (b, qt[s], 0)),
            ],
            out_specs=pl.BlockSpec((1, tq, FS_WIDTH), lambda b, s, qt, kt, ft, lt: (b, qt[s], 0)),
            scratch_shapes=[
                pltpu.VMEM((rows, FS_WIDTH), BF16),
                pltpu.VMEM((1, rows), F32),
                pltpu.VMEM((N_FS_HEADS, HEAD_DIM, tq), F32),
            ],
        ),
        compiler_params=pltpu.CompilerParams(
            dimension_semantics=("parallel", "arbitrary"), vmem_limit_bytes=VMEM_LIMIT),
        name="sb_attention_t",
    )(*tabs, q, k, vt, tri_t, g)


def _mla_t_kernel(qt, kt, ft, lt, q_ref, k_ref, vt_ref, wv_ref, g_ref, o_ref,
                  m_sc, l_sc, acc_sc, *, past, n_kv, tq, tk):
    step = pl.program_id(1)
    qi = qt[step]
    kj = kt[step]
    rows = MLA_HEADS * tq

    @pl.when(ft[step] == 1)
    def _():
        m_sc[...] = jnp.full(m_sc.shape, NEG, F32)
        l_sc[...] = jnp.zeros(l_sc.shape, F32)
        acc_sc[...] = jnp.zeros(acc_sc.shape, F32)

    def body(masked):
        k = k_ref[0]
        vt = vt_ref[0]
        width = MLA_GROUP * tq
        if masked:
            k_pos = kj * tk + lax.broadcasted_iota(jnp.int32, (tk, 1), 0)
            q_pos = _query_positions(past, qi, tq, MLA_GROUP)
            mask = ((k_pos >> CHUNK_SHIFT) <= (q_pos >> CHUNK_SHIFT)) & (k_pos < n_kv)
        for g in range(MLA_HEADS // MLA_GROUP):
            cols = slice(g * width, (g + 1) * width)
            q = q_ref[0, g * MLA_GROUP:(g + 1) * MLA_GROUP].reshape(width, 256)
            s_t = _dot_nt(k, q)
            if masked:
                s_t = jnp.where(mask, s_t, NEG)
            m_prev = m_sc[:, cols]
            m_new = jnp.maximum(m_prev, jnp.max(s_t, axis=0, keepdims=True))
            p_t = jnp.exp2((s_t - m_new) * MLA_EXP2_SCALE)
            alpha = jnp.exp2((m_prev - m_new) * MLA_EXP2_SCALE)
            l_sc[:, cols] = alpha * l_sc[:, cols] + jnp.sum(p_t, axis=0, keepdims=True)
            acc_sc[:, cols] = alpha * acc_sc[:, cols] + _dot(vt, p_t.astype(BF16))
            m_sc[:, cols] = m_new

    @pl.when(lt[step] == 0)
    def _():
        body(False)

    @pl.when(lt[step] == 1)
    def _():
        body(True)
        o_lat_t = acc_sc[...] / l_sc[...]
        o = jnp.zeros((tq, MLA_WIDTH), F32)
        for h in range(MLA_HEADS):
            o = o + _dot(o_lat_t[:, h * tq:(h + 1) * tq].T.astype(BF16), wv_ref[h])
        o_ref[0] = (o * _silu(g_ref[0])).astype(BF16)


def _mla_attention_t(qcat, kcat, ckv_t, wv, g, *, tq, tk):
    nb, _, sq, _ = qcat.shape
    nq = sq // tq
    n_kv = kcat.shape[1]

    def kmax(qi):
        last_chunk = ((qi + 1) * tq - 1) >> CHUNK_SHIFT
        return (min((last_chunk + 1) << CHUNK_SHIFT, n_kv) - 1) // tk

    tabs = _schedule(nq, kmax, False)
    nsteps = int(tabs[0].shape[0])
    rows = MLA_HEADS * tq
    kern = functools.partial(_mla_t_kernel, past=0, n_kv=n_kv, tq=tq, tk=tk)
    return pl.pallas_call(
        kern,
        out_shape=jax.ShapeDtypeStruct((nb, sq, MLA_WIDTH), BF16),
        grid_spec=pltpu.PrefetchScalarGridSpec(
            num_scalar_prefetch=4,
            grid=(nb, nsteps),
            in_specs=[
                pl.BlockSpec((1, MLA_HEADS, tq, 256), lambda b, s, qt, kt, ft, lt: (b, 0, qt[s], 0)),
                pl.BlockSpec((1, tk, 256), lambda b, s, qt, kt, ft, lt: (b, kt[s], 0)),
                pl.BlockSpec((1, KV_LORA, tk), lambda b, s, qt, kt, ft, lt: (b, 0, kt[s])),
                pl.BlockSpec((MLA_HEADS, KV_LORA, MLA_WIDTH), lambda b, s, qt, kt, ft, lt: (0, 0, 0)),
                pl.BlockSpec((1, tq, MLA_WIDTH), lambda b, s, qt, kt, ft, lt: (b, qt[s], 0)),
            ],
            out_specs=pl.BlockSpec((1, tq, MLA_WIDTH), lambda b, s, qt, kt, ft, lt: (b, qt[s], 0)),
            scratch_shapes=[
                pltpu.VMEM((1, rows), F32),
                pltpu.VMEM((1, rows), F32),
                pltpu.VMEM((KV_LORA, rows), F32),
            ],
        ),
        compiler_params=pltpu.CompilerParams(
            dimension_semantics=("parallel", "arbitrary"), vmem_limit_bytes=VMEM_LIMIT),
        name="mla_attention_t",
    )(*tabs, qcat, kcat, ckv_t, wv, g)


def _out_kernel(yf_ref, ys_ref, ym_ref, w_ref, x_ref, mod_ref, gpost_ref, o_ref):
    d = D_MODEL
    y = (_dot(yf_ref[0], w_ref[0:256, :]) + _dot(ys_ref[0], w_ref[256:512, :])
         + _dot(ym_ref[0], w_ref[512:1024, :]))
    gate = mod_ref[0][:, 2 * d:3 * d]
    o_ref[0] = x_ref[0] + gate * _rms(y, gpost_ref[...])


def _output(yf, ys, ym, w_out, x, mod_rows, g_post, tm):
    nb, s, d = x.shape
    mrows = mod_rows.shape[1]
    mblk = 1 if mrows == 1 else tm
    return pl.pallas_call(
        _out_kernel,
        out_shape=jax.ShapeDtypeStruct((nb, s, d), F32),
        grid=(nb, s // tm),
        in_specs=[
            pl.BlockSpec((1, tm, 256), lambda b, i: (b, i, 0)),
            pl.BlockSpec((1, tm, 256), lambda b, i: (b, i, 0)),
            pl.BlockSpec((1, tm, 512), lambda b, i: (b, i, 0)),
            pl.BlockSpec((d, d), lambda b, i: (0, 0)),
            pl.BlockSpec((1, tm, d), lambda b, i: (b, i, 0)),
            pl.BlockSpec((1, mblk, 3 * d),
                         (lambda b, i: (b, 0, 0)) if mrows == 1 else (lambda b, i: (b, i, 0))),
            pl.BlockSpec((1, d), lambda b, i: (0, 0)),
        ],
        out_specs=pl.BlockSpec((1, tm, d), lambda b, i: (b, i, 0)),
        compiler_params=pltpu.CompilerParams(
            dimension_semantics=("parallel", "arbitrary"), vmem_limit_bytes=VMEM_LIMIT),
        name="output_projection",
    )(yf, ys, ym, w_out, x, mod_rows, g_post)


def _layer_weights(l, g_pre, g_post, w_in, b_f, g_q_a, w_uq, g_kv_a, w_uk, w_uv, w_out):
    o = IN_OFFSETS
    w = w_in[l]

    def cols(i):
        return w[:, o[i]:o[i + 1]]

    kpe = cols(11)
    half = MLA_ROPE // 2
    zeros = lambda n: jnp.zeros((D_MODEL, n), F32)
    grp = jnp.concatenate([kpe, cols(3), zeros(64 - MLA_ROPE - 4),
                           -kpe[:, half:], kpe[:, :half], zeros(64 - MLA_ROPE)], axis=1)
    w_main = jnp.concatenate([cols(0), cols(1), cols(2), cols(4), cols(5), cols(6), cols(7), cols(8),
                              cols(9), cols(10), grp, cols(12)], axis=1).astype(BF16)
    bf_row = jnp.zeros((1, LANES), F32).at[0, MLA_ROPE:MLA_ROPE + 4].set(b_f[l])

    uq = w_uq[l].reshape(Q_LORA, MLA_HEADS, MLA_NOPE + MLA_ROPE)
    w_nope = uq[:, :, :MLA_NOPE].reshape(Q_LORA, MLA_HEADS * MLA_NOPE).astype(BF16)
    x1 = uq[:, :, MLA_NOPE:MLA_NOPE + half]
    x2 = uq[:, :, MLA_NOPE + half:]
    zpad = jnp.zeros((Q_LORA, MLA_HEADS, LANES - MLA_ROPE), F32)
    w_xy = jnp.concatenate([x1, x2, zpad, -x2, x1, zpad], axis=2)
    w_xy = jnp.transpose(w_xy, (1, 0, 2)).astype(BF16)

    ukt = jnp.transpose(w_uk[l], (1, 2, 0))
    zk = jnp.zeros_like(ukt)
    even = jnp.concatenate([ukt, zk], axis=1)
    odd = jnp.concatenate([zk, ukt], axis=1)
    is_odd = (jnp.arange(MLA_HEADS) % 2 == 1)[:, None, None]
    w_k = jnp.where(is_odd, odd, even).astype(BF16)

    uvt = jnp.transpose(w_uv[l], (1, 0, 2))
    sel = (jnp.arange(MLA_HEADS)[:, None] == jnp.arange(MLA_HEADS)[None, :]).astype(F32)
    w_v = (uvt[:, :, None, :] * sel[:, None, :, None]).reshape(MLA_HEADS, KV_LORA, MLA_WIDTH)
    return dict(g_pre=g_pre[l][None], g_post=g_post[l][None], w_main=w_main, bf_row=bf_row,
                g_q=g_q_a[l][None], w_nope=w_nope, w_xy=w_xy, w_k=w_k, g_kv=g_kv_a[l][None],
                w_v=w_v.astype(BF16), w_out=w_out[l].astype(BF16))


def _rope_tables(pos):
    half = MLA_ROPE // 2
    inv = ROPE_THETA ** (-jnp.arange(half, dtype=F32) / half)
    ang = pos.astype(F32)[:, None] * inv[None, :]
    pad = jnp.zeros((pos.shape[0], LANES - MLA_ROPE), F32)
    cos = jnp.concatenate([jnp.cos(ang), jnp.cos(ang), pad], axis=1)
    sin = jnp.concatenate([jnp.sin(ang), jnp.sin(ang), pad], axis=1)
    return cos, sin


def _pad_rows(a, total):
    pad = total - a.shape[1]
    if pad == 0:
        return a
    return jnp.concatenate([a, jnp.zeros((a.shape[0], pad) + a.shape[2:], a.dtype)], axis=1)


def _forget_cumsum(logf_all, skv):
    nb, n, _ = logf_all.shape
    length = -(-max(n, skv) // 1024) * 1024
    rows = jnp.transpose(_pad_rows(logf_all, length), (0, 2, 1)).reshape(nb * N_FS_HEADS, length)
    frow = _cumsum_rows(rows).reshape(nb, N_FS_HEADS, length)[:, :, :skv]
    return frow, jnp.transpose(frow, (0, 2, 1))


def _stream_layer(x, mod_rows, cos, sin, lw, past_rows, *, tm, past, fs_tiles, mla_tiles, batch):
    (fq_b, fk_f, fk_b, fv_f, fv_b, fg_f, sq_b, sk_f, sk_b, sv_f, sv_b, sg_f,
     qcat_b, ckv_f, kcat_b, mg_f, gout_f, fv_t, sv_t, ckv_t) = _projection(x, mod_rows, cos, sin, lw, tm)
    nb, rows, _ = x.shape
    seq = nb * rows // batch

    def per_batch(a):
        return a.reshape((batch, seq) + a.shape[2:])

    fq_b, fk_f, fk_b, fv_f, fv_b, fg_f, sq_b, sk_f, sk_b, sv_f, sv_b, sg_f, ckv_f, kcat_b, mg_f, gout_f = [
        per_batch(a) for a in (fq_b, fk_f, fk_b, fv_f, fv_b, fg_f, sq_b, sk_f, sk_b, sv_f, sv_b, sg_f,
                               ckv_f, kcat_b, mg_f, gout_f)]
    qcat_b = jnp.transpose(qcat_b.reshape(nb, MLA_HEADS, batch // nb, seq, 256),
                           (0, 2, 1, 3, 4)).reshape(batch, MLA_HEADS, seq, 256)
    kpe_f = gout_f[:, :, :MLA_ROPE]
    logf = gout_f[:, :, MLA_ROPE:MLA_ROPE + N_FS_HEADS]
    new = (fk_f, fv_f, logf, sk_f, sv_f, ckv_f, kpe_f)

    n_kv = past + seq
    tq_fs, tk_fs = fs_tiles
    tq_m, tk_m = mla_tiles
    skv = -(-n_kv // max(tk_fs, tk_m)) * max(tk_fs, tk_m)
    if past_rows is None:
        fk_a, fv_a, sk_a, sv_a, kc_a, logf_a = fk_b, fv_b, sk_b, sv_b, kcat_b, logf
    else:
        c_fk, c_fv, c_logf, c_sk, c_sv, c_ckv, c_kpe = past_rows

        def join(c, n):
            c = c.reshape(batch, past, -1).astype(BF16)
            return _pad_rows(jnp.concatenate([c, n], axis=1), skv)

        fk_a, fv_a, sk_a, sv_a = join(c_fk, fk_b), join(c_fv, fv_b), join(c_sk, sk_b), join(c_sv, sv_b)
        c_kc = jnp.concatenate([c_ckv, c_kpe, jnp.ones((batch, past, 1), F32),
                                jnp.zeros((batch, past, LANES - MLA_ROPE - 1), F32)], axis=2)
        kc_a = join(c_kc, kcat_b)
        logf_a = jnp.concatenate([c_logf, logf], axis=1)
    frow, fcol = _forget_cumsum(logf_a, skv)

    if past_rows is None:
        y_fox = _fox_attention_t(fq_b, fk_a, fv_t, fcol, frow, fg_f, tq=tq_fs, tk=tk_fs)
        y_sb = _sb_attention_t(sq_b, sk_a, sv_t, sg_f, tq=tq_fs, tk=tk_fs)
        y_mla = _mla_attention_t(qcat_b, kc_a, ckv_t, lw["w_v"], mg_f, tq=tq_m, tk=tk_m)
    else:
        y_fox = _fox_attention(fq_b, fk_a, fv_a, fcol, frow, fg_f, past=past, tq=tq_fs, tk=tk_fs)
        y_sb = _sb_attention(sq_b, sk_a, sv_a, sg_f, past=past, tq=tq_fs, tk=tk_fs)
        y_mla = _mla_attention(qcat_b, kc_a, lw["w_v"], mg_f, past=past, n_kv=n_kv, tq=tq_m, tk=tk_m)

    def per_block(a):
        return a.reshape((nb, rows) + a.shape[2:])

    x_new = _output(per_block(y_fox), per_block(y_sb), per_block(y_mla), lw["w_out"], x, mod_rows,
                    lw["g_post"], tm)
    return x_new, new


def kernel(x_prompt, x_sample, c_prompt, c_sample, cache_fox_k, cache_fox_v, cache_fox_logf, cache_sb_k, cache_sb_v, cache_mla_ckv, cache_mla_kpe, g_pre, g_post, w_ada, b_ada, w_in, b_f, g_q_a, w_uq, g_kv_a, w_uk, w_uv, w_out):
    batch, seq, d = x_prompt.shape
    dec_batch, dec_seq, _ = x_sample.shape
    past_len = cache_fox_k.shape[2]
    dec_rows = dec_batch * dec_seq

    mod = _modulation(jnp.concatenate([c_prompt, c_sample], axis=0), w_ada, b_ada)
    cos_p, sin_p = _rope_tables(jnp.arange(seq, dtype=jnp.int32))
    pos_s = past_len + (jnp.arange(dec_rows, dtype=jnp.int32) % dec_seq)
    cos_s, sin_s = _rope_tables(pos_s)
    skv_s = -(-(past_len + dec_seq) // SB_SUB) * SB_SUB

    y_p = x_prompt
    y_s = x_sample.reshape(1, dec_rows, d)
    rows_p, rows_s = [], []
    for l in range(DEPTH):
        lw = _layer_weights(l, g_pre, g_post, w_in, b_f, g_q_a, w_uq, g_kv_a, w_uk, w_uv, w_out)
        mod_p = mod[l, :batch][:, None, :]
        mod_s = jnp.repeat(mod[l, batch:], dec_seq, axis=0)[None]
        y_p, new_p = _stream_layer(y_p, mod_p, cos_p, sin_p, lw, None, tm=512, past=0,
                                   fs_tiles=(256, 512), mla_tiles=(128, 512), batch=batch)
        past_rows = (cache_fox_k[l], cache_fox_v[l], cache_fox_logf[l], cache_sb_k[l], cache_sb_v[l],
                     cache_mla_ckv[l], cache_mla_kpe[l])
        y_s, new_s = _stream_layer(y_s, mod_s, cos_s, sin_s, lw, past_rows, tm=dec_rows, past=past_len,
                                   fs_tiles=(dec_seq, skv_s), mla_tiles=(dec_seq, skv_s),
                                   batch=dec_batch)
        rows_p.append(new_p)
        rows_s.append(new_s)

    def stack(rows, idx, shape_tail):
        a = jnp.stack([r[idx] for r in rows])
        return a.reshape(a.shape[:3] + shape_tail)

    heads = (N_FS_HEADS, HEAD_DIM)
    tails = (heads, heads, (N_FS_HEADS,), heads, heads, (KV_LORA,), (MLA_ROPE,))
    outs_p = [stack(rows_p, i, t) for i, t in enumerate(tails)]
    outs_s = [stack(rows_s, i, t) for i, t in enumerate(tails)]
    return (y_p, y_s.reshape(dec_batch, dec_seq, d), *outs_p, *outs_s)
```

```python
import functools

import numpy as np
import jax
import jax.numpy as jnp
from jax import lax
from jax.experimental import pallas as pl
from jax.experimental.pallas import tpu as pltpu

D_MODEL = 1024
DEPTH = 2
CHUNK_SHIFT = 6
HEAD_DIM = 64
N_FS_HEADS = 4
FS_WIDTH = N_FS_HEADS * HEAD_DIM
MLA_HEADS = 8
MLA_NOPE = 64
MLA_ROPE = 32
MLA_V = 64
MLA_WIDTH = MLA_HEADS * MLA_V
Q_LORA = 256
KV_LORA = 128
ROPE_THETA = 10000.0
EPS = 1e-6
IN_SPLITS = (256, 256, 256, 4, 256, 256, 256, 256, 256, Q_LORA, KV_LORA, MLA_ROPE, MLA_WIDTH)
IN_OFFSETS = tuple(int(v) for v in np.cumsum((0,) + IN_SPLITS))

LANES = 128
SB_SUB = 256
W_MAIN = 3072
NEG = -1e30
SB_DEAD = -110.0
FOX_DEAD = -110.0
NORM_SLACK = 1.0 + 2.0 ** -7
FS_SCALE = HEAD_DIM ** -0.5
MLA_SCALE = (MLA_NOPE + MLA_ROPE) ** -0.5
MLA_EXP2_SCALE = MLA_SCALE * float(np.log2(np.e))
ONES_LANE = KV_LORA + MLA_ROPE
VMEM_LIMIT = 56 * 1024 * 1024

BF16 = jnp.bfloat16
F32 = jnp.float32


def _dot(a, b):
    return jnp.dot(a, b, preferred_element_type=F32)


def _dot_nt(a, b):
    return lax.dot_general(a, b, (((1,), (1,)), ((), ())), preferred_element_type=F32)


def _silu(g):
    return g / (1.0 + jnp.exp(-g))


def _rms(x, g):
    return x * lax.rsqrt(jnp.mean(x * x, axis=-1, keepdims=True) + EPS) * g


def _mod_kernel(c_ref, w_ref, b_ref, o_ref):
    a = _silu(c_ref[...]).astype(BF16)
    o_ref[0] = _dot(a, w_ref[0].astype(BF16)) + b_ref[0]


def _modulation(c_all, w_ada, b_ada):
    n = c_all.shape[0]
    d = D_MODEL
    return pl.pallas_call(
        _mod_kernel,
        out_shape=jax.ShapeDtypeStruct((DEPTH, n, 3 * d), F32),
        grid=(DEPTH, 3),
        in_specs=[
            pl.BlockSpec((n, d), lambda l, j: (0, 0)),
            pl.BlockSpec((1, d, d), lambda l, j: (l, 0, j)),
            pl.BlockSpec((1, 1, d), lambda l, j: (l, 0, j)),
        ],
        out_specs=pl.BlockSpec((1, n, d), lambda l, j: (l, 0, j)),
        compiler_params=pltpu.CompilerParams(
            dimension_semantics=("arbitrary", "arbitrary"), vmem_limit_bytes=VMEM_LIMIT),
        name="modulation",
    )(c_all, w_ada, b_ada.reshape(DEPTH, 1, 3 * d))


def _proj_kernel(x_ref, mod_ref, gpre_ref, w_ref, bf_ref, gq_ref, wn_ref, wxy_ref, wk_ref,
                 gkv_ref, cos_ref, sin_ref,
                 fq_b, fk_f, fk_b, fv_f, fv_b, fg_f, sq_b, sk_f, sk_b, sv_f, sv_b, sg_f,
                 qcat_b, ckv_f, kcat_b, mg_f, gout_f, fv_t, sv_t, ckv_t):
    d = D_MODEL
    x = x_ref[0]
    mod = mod_ref[0]
    shift = mod[:, 0:d]
    scale = mod[:, d:2 * d]
    h = _rms(x, gpre_ref[...]) * (1.0 + scale) + shift
    hb = h.astype(BF16)

    def proj(a, b):
        return _dot(hb, w_ref[:, a:b])

    fq_b[0] = (proj(0, 256) * FS_SCALE).astype(BF16)
    t = proj(256, 512)
    fk_f[0] = t
    fk_b[0] = t.astype(BF16)
    t = proj(512, 768)
    fv_f[0] = t
    fv_b[0] = t.astype(BF16)
    fv_t[0] = t.T.astype(BF16)
    fg_f[0] = proj(768, 1024)
    sq_b[0] = (proj(1024, 1280) * FS_SCALE).astype(BF16)
    t = proj(1280, 1536)
    sk_f[0] = t
    sk_b[0] = t.astype(BF16)
    t = proj(1536, 1792)
    sv_f[0] = t
    sv_b[0] = t.astype(BF16)
    sv_t[0] = t.T.astype(BF16)
    sg_f[0] = proj(1792, 2048)
    mg_f[0] = proj(2560, 3072)

    cos = cos_ref[...]
    sin = sin_ref[...]

    cg = proj(2304, 2560)
    ckvn = _rms(cg[:, 0:KV_LORA], gkv_ref[...])
    ckv_f[0] = ckvn
    ckv_t[0] = ckvn.T.astype(BF16)
    grp = cg[:, KV_LORA:2 * KV_LORA]
    rope_k = grp * cos + pltpu.roll(grp, 64, axis=1) * sin
    zf = grp + bf_ref[...]
    logf = jnp.minimum(zf, 0.0) - jnp.log(1.0 + jnp.exp(-jnp.abs(zf)))
    lane = lax.broadcasted_iota(jnp.int32, grp.shape, 1)
    gout_f[0] = jnp.where(lane < MLA_ROPE, rope_k, logf)
    ones_lane = jnp.where(lane == ONES_LANE - KV_LORA, 1.0, 0.0)
    kcat_b[0] = jnp.concatenate([ckvn, rope_k + ones_lane], axis=1).astype(BF16)

    cqn = _rms(proj(2048, 2304), gq_ref[...]).astype(BF16)
    qn = _dot(cqn, wn_ref[...]).astype(BF16)
    for hh in range(MLA_HEADS):
        pair = qn[:, (hh // 2) * LANES:(hh // 2 + 1) * LANES]
        qlat = _dot(pair, wk_ref[hh])
        xy = _dot(cqn, wxy_ref[hh])
        rope_q = xy[:, 0:LANES] * cos + xy[:, LANES:2 * LANES] * sin
        qcat_b[0, hh] = jnp.concatenate([qlat, rope_q], axis=1).astype(BF16)


def _projection(x, mod_rows, cos, sin, lw, tm):
    nb, s, d = x.shape
    mrows = mod_rows.shape[1]
    mblk = 1 if mrows == 1 else tm
    grid = (nb, s // tm)

    def row(width, dtype):
        return (jax.ShapeDtypeStruct((nb, s, width), dtype),
                pl.BlockSpec((1, tm, width), lambda b, i: (b, i, 0)))

    def col(width):
        return (jax.ShapeDtypeStruct((nb, width, s), BF16),
                pl.BlockSpec((1, width, tm), lambda b, i: (b, 0, i)))

    outs = [row(256, BF16), row(256, F32), row(256, BF16), row(256, F32), row(256, BF16),
            row(256, F32), row(256, BF16), row(256, F32), row(256, BF16), row(256, F32),
            row(256, BF16), row(256, F32),
            (jax.ShapeDtypeStruct((nb, MLA_HEADS, s, 256), BF16),
             pl.BlockSpec((1, MLA_HEADS, tm, 256), lambda b, i: (b, 0, i, 0))),
            row(KV_LORA, F32), row(256, BF16), row(MLA_WIDTH, F32), row(LANES, F32),
            col(256), col(256), col(KV_LORA)]

    def full(a):
        nd = a.ndim
        return pl.BlockSpec(a.shape, lambda b, i: (0,) * nd)

    consts = [lw["g_pre"], lw["w_main"], lw["bf_row"], lw["g_q"], lw["w_nope"], lw["w_xy"],
              lw["w_k"], lw["g_kv"]]
    in_specs = ([pl.BlockSpec((1, tm, d), lambda b, i: (b, i, 0)),
                 pl.BlockSpec((1, mblk, 3 * d),
                              (lambda b, i: (b, 0, 0)) if mrows == 1 else (lambda b, i: (b, i, 0)))]
                + [full(a) for a in consts]
                + [pl.BlockSpec((tm, LANES), lambda b, i: (i, 0)),
                   pl.BlockSpec((tm, LANES), lambda b, i: (i, 0))])
    return pl.pallas_call(
        _proj_kernel,
        out_shape=[o[0] for o in outs],
        grid=grid,
        in_specs=in_specs,
        out_specs=[o[1] for o in outs],
        compiler_params=pltpu.CompilerParams(
            dimension_semantics=("parallel", "arbitrary"), vmem_limit_bytes=VMEM_LIMIT),
        name="projection",
    )(x, mod_rows, *consts, cos, sin)


def _cumsum_kernel(x_ref, u_ref, l_ref, o_ref):
    x = x_ref[0]
    w = jnp.dot(x, u_ref[...], preferred_element_type=F32, precision=lax.Precision.HIGHEST)
    tot = jnp.broadcast_to(w[:, LANES - 1:LANES], w.shape)
    off = jnp.dot(l_ref[...], tot, preferred_element_type=F32, precision=lax.Precision.HIGHEST)
    o_ref[0] = w + off


def _cumsum_rows(x):
    r, length = x.shape
    n = length // LANES
    u = jnp.asarray(np.triu(np.ones((LANES, LANES), np.float32)))
    lo = jnp.asarray(np.tril(np.ones((n, n), np.float32), -1))
    out = pl.pallas_call(
        _cumsum_kernel,
        out_shape=jax.ShapeDtypeStruct((r, n, LANES), F32),
        grid=(r,),
        in_specs=[pl.BlockSpec((1, n, LANES), lambda i: (i, 0, 0)),
                  pl.BlockSpec((LANES, LANES), lambda i: (0, 0)),
                  pl.BlockSpec((n, n), lambda i: (0, 0))],
        out_specs=pl.BlockSpec((1, n, LANES), lambda i: (i, 0, 0)),
        compiler_params=pltpu.CompilerParams(dimension_semantics=("arbitrary",)),
        name="cumsum_logf",
    )(x.reshape(r, n, LANES), u, lo)
    return out.reshape(r, length)


def _schedule(nq, kmax_fn, reverse):
    qi_l, kj_l, first_l, last_l = [], [], [], []
    for qi in range(nq):
        ks = list(range(kmax_fn(qi) + 1))
        if reverse:
            ks = ks[::-1]
        for n, kj in enumerate(ks):
            qi_l.append(qi)
            kj_l.append(kj)
            first_l.append(int(n == 0))
            last_l.append(int(n == len(ks) - 1))
    return tuple(jnp.asarray(np.asarray(a, np.int32)) for a in (qi_l, kj_l, first_l, last_l))


def _head_lane_mask(shape, h):
    lane = lax.broadcasted_iota(jnp.int32, shape, 1)
    return (lane >= HEAD_DIM * h) & (lane < HEAD_DIM * (h + 1))


def _fox_kernel(qt, kt, ft, lt, q_ref, k_ref, v_ref, fc_ref, fr_ref, g_ref, o_ref,
                qm_sc, m_sc, l_sc, acc_sc, *, past, tq, tk):
    step = pl.program_id(1)
    qi = qt[step]
    kj = kt[step]

    @pl.when(ft[step] == 1)
    def _():
        q = q_ref[0]
        for h in range(N_FS_HEADS):
            qm_sc[h] = jnp.where(_head_lane_mask(q.shape, h), q, jnp.zeros_like(q))
        m_sc[...] = jnp.full(m_sc.shape, NEG, F32)
        l_sc[...] = jnp.zeros(l_sc.shape, F32)
        acc_sc[...] = jnp.zeros(acc_sc.shape, F32)

    k = k_ref[0]
    v = v_ref[0]
    q_pos = past + qi * tq + lax.broadcasted_iota(jnp.int32, (tq, 1), 0)
    k_pos = kj * tk + lax.broadcasted_iota(jnp.int32, (1, tk), 1)
    mask = k_pos <= q_pos
    fc = fc_ref[0]
    fr = fr_ref[0]
    for h in range(N_FS_HEADS):
        s = _dot_nt(qm_sc[h], k)
        s = s + (fc[:, h:h + 1] - fr[h:h + 1, :])
        s = jnp.where(mask, s, NEG)
        m_prev = m_sc[h]
        m_new = jnp.maximum(m_prev, jnp.max(s, axis=1, keepdims=True))
        p = jnp.exp(s - m_new)
        alpha = jnp.exp(m_prev - m_new)
        l_sc[h] = alpha * l_sc[h] + jnp.sum(p, axis=1, keepdims=True)
        acc_sc[h] = alpha * acc_sc[h] + _dot(p.astype(BF16), v)
        m_sc[h] = m_new

    @pl.when(lt[step] == 1)
    def _():
        o = jnp.zeros((tq, FS_WIDTH), F32)
        for h in range(N_FS_HEADS):
            o = jnp.where(_head_lane_mask(o.shape, h), acc_sc[h] / l_sc[h], o)
        o_ref[0] = (o * _silu(g_ref[0])).astype(BF16)


def _fox_attention(q, k, v, fcol, frow, g, *, past, tq, tk):
    nb, sq, _ = q.shape
    nq = sq // tq
    tabs = _schedule(nq, lambda qi: (past + (qi + 1) * tq - 1) // tk, False)
    nsteps = int(tabs[0].shape[0])
    qoff = past // tq
    kern = functools.partial(_fox_kernel, past=past, tq=tq, tk=tk)
    return pl.pallas_call(
        kern,
        out_shape=jax.ShapeDtypeStruct((nb, sq, FS_WIDTH), BF16),
        grid_spec=pltpu.PrefetchScalarGridSpec(
            num_scalar_prefetch=4,
            grid=(nb, nsteps),
            in_specs=[
                pl.BlockSpec((1, tq, FS_WIDTH), lambda b, s, qt, kt, ft, lt: (b, qt[s], 0)),
                pl.BlockSpec((1, tk, FS_WIDTH), lambda b, s, qt, kt, ft, lt: (b, kt[s], 0)),
                pl.BlockSpec((1, tk, FS_WIDTH), lambda b, s, qt, kt, ft, lt: (b, kt[s], 0)),
                pl.BlockSpec((1, tq, N_FS_HEADS), lambda b, s, qt, kt, ft, lt: (b, qoff + qt[s], 0)),
                pl.BlockSpec((1, N_FS_HEADS, tk), lambda b, s, qt, kt, ft, lt: (b, 0, kt[s])),
                pl.BlockSpec((1, tq, FS_WIDTH), lambda b, s, qt, kt, ft, lt: (b, qt[s], 0)),
            ],
            out_specs=pl.BlockSpec((1, tq, FS_WIDTH), lambda b, s, qt, kt, ft, lt: (b, qt[s], 0)),
            scratch_shapes=[
                pltpu.VMEM((N_FS_HEADS, tq, FS_WIDTH), BF16),
                pltpu.VMEM((N_FS_HEADS, tq, 1), F32),
                pltpu.VMEM((N_FS_HEADS, tq, 1), F32),
                pltpu.VMEM((N_FS_HEADS, tq, FS_WIDTH), F32),
            ],
        ),
        compiler_params=pltpu.CompilerParams(
            dimension_semantics=("parallel", "arbitrary"), vmem_limit_bytes=VMEM_LIMIT),
        name="fox_attention",
    )(*tabs, q, k, v, fcol, frow, g)


def _sb_kernel(qt, kt, ft, lt, q_ref, k_ref, v_ref, tri_ref, g_ref, o_ref,
               qm_sc, r_sc, acc_sc, *, past, tq, tk):
    step = pl.program_id(1)
    qi = qt[step]
    kj = kt[step]

    @pl.when(ft[step] == 1)
    def _():
        q = q_ref[0]
        for h in range(N_FS_HEADS):
            qm_sc[h] = jnp.where(_head_lane_mask(q.shape, h), q, jnp.zeros_like(q))
        r_sc[...] = jnp.zeros(r_sc.shape, F32)
        acc_sc[...] = jnp.zeros(acc_sc.shape, F32)

    tri = tri_ref[...]
    q_pos = past + qi * tq + lax.broadcasted_iota(jnp.int32, (tq, 1), 0)
    for sub in reversed(range(tk // SB_SUB)):
        k = k_ref[0, sub * SB_SUB:(sub + 1) * SB_SUB, :]
        v = v_ref[0, sub * SB_SUB:(sub + 1) * SB_SUB, :]
        k_pos = kj * tk + sub * SB_SUB + lax.broadcasted_iota(jnp.int32, (1, SB_SUB), 1)
        mask = k_pos < q_pos
        for h in range(N_FS_HEADS):
            z = _dot_nt(qm_sc[h], k)
            sp = jnp.maximum(z, 0.0) + jnp.log(1.0 + jnp.exp(-jnp.abs(z)))
            l1m = jnp.where(mask, -sp, 0.0)
            hi = l1m.astype(BF16)
            lo = (l1m - hi.astype(F32)).astype(BF16)
            c = _dot(hi, tri) + _dot(lo, tri)
            r = r_sc[h]
            a = jnp.where(mask, jnp.exp(z + (c + r)), 0.0)
            acc_sc[h] = acc_sc[h] + _dot(a.astype(BF16), v)
            r_sc[h] = r + c[:, 0:1]

    @pl.when(lt[step] == 1)
    def _():
        o = jnp.zeros((tq, FS_WIDTH), F32)
        for h in range(N_FS_HEADS):
            o = jnp.where(_head_lane_mask(o.shape, h), acc_sc[h], o)
        o_ref[0] = (o * _silu(g_ref[0])).astype(BF16)


def _sb_attention(q, k, v, g, *, past, tq, tk):
    nb, sq, _ = q.shape
    nq = sq // tq
    tabs = _schedule(nq, lambda qi: (past + (qi + 1) * tq - 1) // tk, True)
    nsteps = int(tabs[0].shape[0])
    tri = jnp.asarray(np.tril(np.ones((SB_SUB, SB_SUB), np.float32))).astype(BF16)
    kern = functools.partial(_sb_kernel, past=past, tq=tq, tk=tk)
    return pl.pallas_call(
        kern,
        out_shape=jax.ShapeDtypeStruct((nb, sq, FS_WIDTH), BF16),
        grid_spec=pltpu.PrefetchScalarGridSpec(
            num_scalar_prefetch=4,
            grid=(nb, nsteps),
            in_specs=[
                pl.BlockSpec((1, tq, FS_WIDTH), lambda b, s, qt, kt, ft, lt: (b, qt[s], 0)),
                pl.BlockSpec((1, tk, FS_WIDTH), lambda b, s, qt, kt, ft, lt: (b, kt[s], 0)),
                pl.BlockSpec((1, tk, FS_WIDTH), lambda b, s, qt, kt, ft, lt: (b, kt[s], 0)),
                pl.BlockSpec((SB_SUB, SB_SUB), lambda b, s, qt, kt, ft, lt: (0, 0)),
                pl.BlockSpec((1, tq, FS_WIDTH), lambda b, s, qt, kt, ft, lt: (b, qt[s], 0)),
            ],
            out_specs=pl.BlockSpec((1, tq, FS_WIDTH), lambda b, s, qt, kt, ft, lt: (b, qt[s], 0)),
            scratch_shapes=[
                pltpu.VMEM((N_FS_HEADS, tq, FS_WIDTH), BF16),
                pltpu.VMEM((N_FS_HEADS, tq, 1), F32),
                pltpu.VMEM((N_FS_HEADS, tq, FS_WIDTH), F32),
            ],
        ),
        compiler_params=pltpu.CompilerParams(
            dimension_semantics=("parallel", "arbitrary"), vmem_limit_bytes=VMEM_LIMIT),
        name="sb_attention",
    )(*tabs, q, k, v, tri, g)


def _mla_kernel(qt, kt, ft, lt, q_ref, k_ref, wv_ref, g_ref, o_ref,
                m_sc, acc_sc, *, past, n_kv, tq, tk):
    step = pl.program_id(1)
    qi = qt[step]
    kj = kt[step]
    rows = MLA_HEADS * tq

    @pl.when(ft[step] == 1)
    def _():
        m_sc[...] = jnp.full(m_sc.shape, NEG, F32)
        acc_sc[...] = jnp.zeros(acc_sc.shape, F32)

    k = k_ref[0]
    q_pos = past + qi * tq + lax.broadcasted_iota(jnp.int32, (tq, 1), 0)
    k_pos = kj * tk + lax.broadcasted_iota(jnp.int32, (1, tk), 1)
    mask = ((k_pos >> CHUNK_SHIFT) <= (q_pos >> CHUNK_SHIFT)) & (k_pos < n_kv)
    s = _dot_nt(q_ref[0].reshape(rows, 256), k).reshape(MLA_HEADS, tq, tk)
    s = jnp.where(mask[None], s, NEG)
    m_prev = m_sc[...]
    m_new = jnp.maximum(m_prev, jnp.max(s, axis=2, keepdims=True))
    p = jnp.exp2((s - m_new) * MLA_EXP2_SCALE)
    alpha = jnp.exp2((m_prev - m_new) * MLA_EXP2_SCALE)
    pv = _dot(p.reshape(rows, tk).astype(BF16), k)
    acc_sc[...] = alpha.reshape(rows, 1) * acc_sc[...] + pv
    m_sc[...] = m_new

    @pl.when(lt[step] == 1)
    def _():
        acc = acc_sc[...]
        o_lat = (acc[:, 0:KV_LORA] / acc[:, ONES_LANE:ONES_LANE + 1]).astype(BF16)
        o = jnp.zeros((tq, MLA_WIDTH), F32)
        for h in range(MLA_HEADS):
            o = o + _dot(o_lat[h * tq:(h + 1) * tq], wv_ref[h])
        o_ref[0] = (o * _silu(g_ref[0])).astype(BF16)


def _mla_attention(qcat, kcat, wv, g, *, past, n_kv, tq, tk):
    nb, _, sq, _ = qcat.shape
    nq = sq // tq

    def kmax(qi):
        last_q = past + (qi + 1) * tq - 1
        end = min(((last_q >> CHUNK_SHIFT) + 1) << CHUNK_SHIFT, n_kv)
        return (end - 1) // tk

    tabs = _schedule(nq, kmax, False)
    nsteps = int(tabs[0].shape[0])
    kern = functools.partial(_mla_kernel, past=past, n_kv=n_kv, tq=tq, tk=tk)
    return pl.pallas_call(
        kern,
        out_shape=jax.ShapeDtypeStruct((nb, sq, MLA_WIDTH), BF16),
        grid_spec=pltpu.PrefetchScalarGridSpec(
            num_scalar_prefetch=4,
            grid=(nb, nsteps),
            in_specs=[
                pl.BlockSpec((1, MLA_HEADS, tq, 256), lambda b, s, qt, kt, ft, lt: (b, 0, qt[s], 0)),
                pl.BlockSpec((1, tk, 256), lambda b, s, qt, kt, ft, lt: (b, kt[s], 0)),
                pl.BlockSpec((MLA_HEADS, KV_LORA, MLA_WIDTH), lambda b, s, qt, kt, ft, lt: (0, 0, 0)),
                pl.BlockSpec((1, tq, MLA_WIDTH), lambda b, s, qt, kt, ft, lt: (b, qt[s], 0)),
            ],
            out_specs=pl.BlockSpec((1, tq, MLA_WIDTH), lambda b, s, qt, kt, ft, lt: (b, qt[s], 0)),
            scratch_shapes=[
                pltpu.VMEM((MLA_HEADS, tq, 1), F32),
                pltpu.VMEM((MLA_HEADS * tq, 256), F32),
            ],
        ),
        compiler_params=pltpu.CompilerParams(
            dimension_semantics=("parallel", "arbitrary"), vmem_limit_bytes=VMEM_LIMIT),
        name="mla_attention",
    )(*tabs, qcat, kcat, wv, g)


def _stack_heads(q, qm_sc, tq):
    for h in range(N_FS_HEADS):
        qm_sc[h * tq:(h + 1) * tq, :] = jnp.where(_head_lane_mask(q.shape, h), q, jnp.zeros_like(q))


def _query_positions(past, qi, tq, heads):
    lane = lax.broadcasted_iota(jnp.int32, (1, heads * tq), 1)
    return past + qi * tq + (lane & (tq - 1))


def _log2(n):
    assert n & (n - 1) == 0
    return n.bit_length() - 1


def _live_flag(j, value, floor):
    return jnp.logical_and(j >= 0, value > floor).astype(jnp.int32)


def _fox_t_kernel(q_ref, k_ref, vt_ref, fc_ref, fr_ref, g_ref, e_ref, o_ref,
                  qm_sc, fkb_sc, pm_sc, fq_sc, qn_sc, m_sc, l_sc, acc_sc, *, tq, tk):
    qi = pl.program_id(1)
    seq = k_ref.shape[1]
    groups = tq // LANES

    def per_head_row(x):
        return jnp.concatenate([x[:, (g // groups) * LANES:(g // groups + 1) * LANES]
                                for g in range(N_FS_HEADS * groups)], axis=1)

    @pl.when(qi == 0)
    def _():
        fc = fc_ref[0]
        for h in range(N_FS_HEADS):
            fkb_sc[h] = jnp.broadcast_to(fc[:, h:h + 1], (seq, LANES))
        running = jnp.zeros((1, N_FS_HEADS * LANES), F32)
        for j in range(seq // tk):
            kf = k_ref[0, j * tk:(j + 1) * tk, :].astype(F32)
            n2 = _dot((kf * kf).astype(BF16), e_ref[...])
            running = jnp.maximum(running, jnp.max(n2, axis=0, keepdims=True))
            pm_sc[j:j + 1, :] = running

    _stack_heads(q_ref[0], qm_sc, tq)
    fr = fr_ref[0]
    fq_sc[...] = jnp.concatenate([fr[h:h + 1, :] for h in range(N_FS_HEADS)], axis=1)
    qf = qm_sc[...].astype(F32)
    qn_sc[...] = _dot_nt(jnp.ones((8, FS_WIDTH), BF16), (qf * qf).astype(BF16))[0:1, :]
    m_sc[...] = jnp.full(m_sc.shape, NEG, F32)
    l_sc[...] = jnp.zeros(l_sc.shape, F32)
    acc_sc[...] = jnp.zeros(acc_sc.shape, F32)

    def tile(j, masked):
        start = pl.multiple_of(j * tk, tk)
        s_t = _dot_nt(k_ref[0, pl.ds(start, tk), :], qm_sc[...])
        fk = fkb_sc[:, pl.ds(start, tk), :]
        s_t = jnp.concatenate(
            [s_t[:, g * LANES:(g + 1) * LANES] - fk[g // groups] for g in range(N_FS_HEADS * groups)],
            axis=1) + fq_sc[...]
        if masked:
            k_pos = j * tk + lax.broadcasted_iota(jnp.int32, (tk, 1), 0)
            s_t = jnp.where(k_pos <= _query_positions(0, qi, tq, N_FS_HEADS), s_t, NEG)
        m_prev = m_sc[...]
        m_new = jnp.maximum(m_prev, jnp.max(s_t, axis=0, keepdims=True))
        p_t = jnp.exp(s_t - m_new)
        alpha = jnp.exp(m_prev - m_new)
        l_sc[...] = alpha * l_sc[...] + jnp.sum(p_t, axis=0, keepdims=True)
        m_sc[...] = m_new
        p_b = p_t.astype(BF16)
        for h in range(N_FS_HEADS):
            pv = _dot(vt_ref[0, h * HEAD_DIM:(h + 1) * HEAD_DIM, pl.ds(start, tk)],
                      p_b[:, h * tq:(h + 1) * tq])
            acc_sc[h] = alpha[:, h * tq:(h + 1) * tq] * acc_sc[h] + pv

    def live(j):
        jc = jnp.maximum(j, 0)
        pm = per_head_row(pm_sc[pl.ds(jc, 1), :])
        fend = per_head_row(jnp.concatenate(
            [fkb_sc[h, pl.ds((jc + 1) * tk - 1, 1), :] for h in range(N_FS_HEADS)], axis=1))
        bound = jnp.sqrt(qn_sc[...] * pm) * NORM_SLACK + fq_sc[...] - fend - m_sc[...]
        return _live_flag(j, jnp.max(bound), FOX_DEAD)

    j_diag = ((qi + 1) * tq - 1) >> _log2(tk)
    tile(j_diag, True)

    def step(carry):
        j = carry[0]
        tile(j, False)
        return j - 1, live(j - 1)

    lax.while_loop(lambda c: c[1] > 0, step, (j_diag - 1, live(j_diag - 1)))

    l = l_sc[...]
    o_t = jnp.concatenate([acc_sc[h] / l[:, h * tq:(h + 1) * tq] for h in range(N_FS_HEADS)], axis=0)
    o_ref[0] = (o_t.T * _silu(g_ref[0])).astype(BF16)


def _fox_attention_t(q, k, vt, fcol, frow, g, *, tq, tk):
    nb, sq, _ = q.shape
    rows = N_FS_HEADS * tq
    head_of_col = np.arange(FS_WIDTH)[:, None] // HEAD_DIM
    head_of_lane = np.arange(N_FS_HEADS * LANES)[None, :] // LANES
    expand = jnp.asarray((head_of_col == head_of_lane).astype(np.float32)).astype(BF16)
    kern = functools.partial(_fox_t_kernel, tq=tq, tk=tk)
    return pl.pallas_call(
        kern,
        out_shape=jax.ShapeDtypeStruct((nb, sq, FS_WIDTH), BF16),
        grid=(nb, sq // tq),
        in_specs=[
            pl.BlockSpec((1, tq, FS_WIDTH), lambda b, i: (b, i, 0)),
            pl.BlockSpec((1, sq, FS_WIDTH), lambda b, i: (b, 0, 0)),
            pl.BlockSpec((1, FS_WIDTH, sq), lambda b, i: (b, 0, 0)),
            pl.BlockSpec((1, sq, N_FS_HEADS), lambda b, i: (b, 0, 0)),
            pl.BlockSpec((1, N_FS_HEADS, tq), lambda b, i: (b, 0, i)),
            pl.BlockSpec((1, tq, FS_WIDTH), lambda b, i: (b, i, 0)),
            pl.BlockSpec(expand.shape, lambda b, i: (0, 0)),
        ],
        out_specs=pl.BlockSpec((1, tq, FS_WIDTH), lambda b, i: (b, i, 0)),
        scratch_shapes=[
            pltpu.VMEM((rows, FS_WIDTH), BF16),
            pltpu.VMEM((N_FS_HEADS, sq, LANES), F32),
            pltpu.VMEM((sq // tk, N_FS_HEADS * LANES), F32),
            pltpu.VMEM((1, rows), F32),
            pltpu.VMEM((1, rows), F32),
            pltpu.VMEM((1, rows), F32),
            pltpu.VMEM((1, rows), F32),
            pltpu.VMEM((N_FS_HEADS, HEAD_DIM, tq), F32),
        ],
        compiler_params=pltpu.CompilerParams(
            dimension_semantics=("parallel", "arbitrary"), vmem_limit_bytes=VMEM_LIMIT),
        name="fox_attention_t",
    )(q, k, vt, fcol, frow, g, expand)


def _sb_t_kernel(q_ref, k_ref, vt_ref, tri_ref, g_ref, o_ref, qm_sc, r_sc, acc_sc, *, tq, tk):
    qi = pl.program_id(1)
    _stack_heads(q_ref[0], qm_sc, tq)
    r_sc[...] = jnp.zeros(r_sc.shape, F32)
    acc_sc[...] = jnp.zeros(acc_sc.shape, F32)

    def tile(j, masked):
        start = pl.multiple_of(j * tk, tk)
        z_all = _dot_nt(k_ref[0, pl.ds(start, tk), :], qm_sc[...])
        tri_t = tri_ref[...]
        for sub in reversed(range(tk // SB_SUB)):
            z = z_all[sub * SB_SUB:(sub + 1) * SB_SUB, :]
            l1m = -(jnp.maximum(z, 0.0) + jnp.log(1.0 + jnp.exp(-jnp.abs(z))))
            if masked:
                k_pos = j * tk + sub * SB_SUB + lax.broadcasted_iota(jnp.int32, (SB_SUB, 1), 0)
                mask = k_pos < _query_positions(0, qi, tq, N_FS_HEADS)
                l1m = jnp.where(mask, l1m, 0.0)
            hi = l1m.astype(BF16)
            lo = (l1m - hi.astype(F32)).astype(BF16)
            c = _dot(tri_t, hi) + _dot(tri_t, lo)
            r = r_sc[...]
            a = jnp.exp(z + (c + r))
            if masked:
                a = jnp.where(mask, a, 0.0)
            a_b = a.astype(BF16)
            for h in range(N_FS_HEADS):
                acc_sc[h] = acc_sc[h] + _dot(
                    vt_ref[0, h * HEAD_DIM:(h + 1) * HEAD_DIM,
                           pl.ds(pl.multiple_of(start + sub * SB_SUB, SB_SUB), SB_SUB)],
                    a_b[:, h * tq:(h + 1) * tq])
            r_sc[...] = r + c[0:1, :]

    def live(j):
        return _live_flag(j, jnp.max(r_sc[...]), SB_DEAD)

    j_diag = ((qi + 1) * tq - 1) >> _log2(tk)
    tile(j_diag, True)

    def step(carry):
        j = carry[0]
        tile(j, False)
        return j - 1, live(j - 1)

    lax.while_loop(lambda c: c[1] > 0, step, (j_diag - 1, live(j_diag - 1)))

    o_t = jnp.concatenate([acc_sc[h] for h in range(N_FS_HEADS)], axis=0)
    o_ref[0] = (o_t.T * _silu(g_ref[0])).astype(BF16)


def _sb_attention_t(q, k, vt, g, *, tq, tk):
    nb, sq, _ = q.shape
    rows = N_FS_HEADS * tq
    tri_t = jnp.asarray(np.triu(np.ones((SB_SUB, SB_SUB), np.float32))).astype(BF16)
    kern = functools.partial(_sb_t_kernel, tq=tq, tk=tk)
    return pl.pallas_call(
        kern,
        out_shape=jax.ShapeDtypeStruct((nb, sq, FS_WIDTH), BF16),
        grid=(nb, sq // tq),
        in_specs=[
            pl.BlockSpec((1, tq, FS_WIDTH), lambda b, i: (b, i, 0)),
            pl.BlockSpec((1, sq, FS_WIDTH), lambda b, i: (b, 0, 0)),
            pl.BlockSpec((1, FS_WIDTH, sq), lambda b, i: (b, 0, 0)),
            pl.BlockSpec((SB_SUB, SB_SUB), lambda b, i: (0, 0)),
            pl.BlockSpec((1, tq, FS_WIDTH), lambda b, i: (b, i, 0)),
        ],
        out_specs=pl.BlockSpec((1, tq, FS_WIDTH), lambda b, i: (b, i, 0)),
        scratch_shapes=[
            pltpu.VMEM((rows, FS_WIDTH), BF16),
            pltpu.VMEM((1, rows), F32),
            pltpu.VMEM((N_FS_HEADS, HEAD_DIM, tq), F32),
        ],
        compiler_params=pltpu.CompilerParams(
            dimension_semantics=("parallel", "arbitrary"), vmem_limit_bytes=VMEM_LIMIT),
        name="sb_attention_t",
    )(q, k, vt, tri_t, g)


def _mla_t_kernel(q_ref, k_ref, vt_ref, wv_ref, g_ref, o_ref, m_sc, l_sc, acc_sc, *, tq, tk):
    qi = pl.program_id(1)
    rows = MLA_HEADS * tq
    m_sc[...] = jnp.full(m_sc.shape, NEG, F32)
    l_sc[...] = jnp.zeros(l_sc.shape, F32)
    acc_sc[...] = jnp.zeros(acc_sc.shape, F32)

    def tile(j, masked):
        start = pl.multiple_of(j * tk, tk)
        s_t = _dot_nt(k_ref[0, pl.ds(start, tk), :], q_ref[0].reshape(rows, 256))
        if masked:
            k_pos = j * tk + lax.broadcasted_iota(jnp.int32, (tk, 1), 0)
            q_pos = _query_positions(0, qi, tq, MLA_HEADS)
            s_t = jnp.where((k_pos >> CHUNK_SHIFT) <= (q_pos >> CHUNK_SHIFT), s_t, NEG)
        m_prev = m_sc[...]
        m_new = jnp.maximum(m_prev, jnp.max(s_t, axis=0, keepdims=True))
        p_t = jnp.exp2((s_t - m_new) * MLA_EXP2_SCALE)
        alpha = jnp.exp2((m_prev - m_new) * MLA_EXP2_SCALE)
        l_sc[...] = alpha * l_sc[...] + jnp.sum(p_t, axis=0, keepdims=True)
        acc_sc[...] = alpha * acc_sc[...] + _dot(vt_ref[0, :, pl.ds(start, tk)], p_t.astype(BF16))
        m_sc[...] = m_new

    visible_end = ((((qi + 1) * tq - 1) >> CHUNK_SHIFT) + 1) << CHUNK_SHIFT
    j_last = (visible_end - 1) >> _log2(tk)

    def step(j, carry):
        tile(j, False)
        return carry

    lax.fori_loop(0, j_last, step, 0)
    tile(j_last, True)

    o_lat_t = acc_sc[...] / l_sc[...]
    o = jnp.zeros((tq, MLA_WIDTH), F32)
    for h in range(MLA_HEADS):
        o = o + _dot(o_lat_t[:, h * tq:(h + 1) * tq].T.astype(BF16), wv_ref[h])
    o_ref[0] = (o * _silu(g_ref[0])).astype(BF16)


def _mla_attention_t(qcat, kcat, ckv_t, wv, g, *, tq, tk):
    nb, _, sq, _ = qcat.shape
    rows = MLA_HEADS * tq
    kern = functools.partial(_mla_t_kernel, tq=tq, tk=tk)
    return pl.pallas_call(
        kern,
        out_shape=jax.ShapeDtypeStruct((nb, sq, MLA_WIDTH), BF16),
        grid=(nb, sq // tq),
        in_specs=[
            pl.BlockSpec((1, MLA_HEADS, tq, 256), lambda b, i: (b, 0, i, 0)),
            pl.BlockSpec((1, sq, 256), lambda b, i: (b, 0, 0)),
            pl.BlockSpec((1, KV_LORA, sq), lambda b, i: (b, 0, 0)),
            pl.BlockSpec((MLA_HEADS, KV_LORA, MLA_WIDTH), lambda b, i: (0, 0, 0)),
            pl.BlockSpec((1, tq, MLA_WIDTH), lambda b, i: (b, i, 0)),
        ],
        out_specs=pl.BlockSpec((1, tq, MLA_WIDTH), lambda b, i: (b, i, 0)),
        scratch_shapes=[
            pltpu.VMEM((1, rows), F32),
            pltpu.VMEM((1, rows), F32),
            pltpu.VMEM((KV_LORA, rows), F32),
        ],
        compiler_params=pltpu.CompilerParams(
            dimension_semantics=("parallel", "arbitrary"), vmem_limit_bytes=VMEM_LIMIT),
        name="mla_attention_t",
    )(qcat, kcat, ckv_t, wv, g)


def _out_kernel(yf_ref, ys_ref, ym_ref, w_ref, x_ref, mod_ref, gpost_ref, o_ref):
    d = D_MODEL
    y = (_dot(yf_ref[0], w_ref[0:256, :]) + _dot(ys_ref[0], w_ref[256:512, :])
         + _dot(ym_ref[0], w_ref[512:1024, :]))
    gate = mod_ref[0][:, 2 * d:3 * d]
    o_ref[0] = x_ref[0] + gate * _rms(y, gpost_ref[...])


def _output(yf, ys, ym, w_out, x, mod_rows, g_post, tm):
    nb, s, d = x.shape
    mrows = mod_rows.shape[1]
    mblk = 1 if mrows == 1 else tm
    return pl.pallas_call(
        _out_kernel,
        out_shape=jax.ShapeDtypeStruct((nb, s, d), F32),
        grid=(nb, s // tm),
        in_specs=[
            pl.BlockSpec((1, tm, 256), lambda b, i: (b, i, 0)),
            pl.BlockSpec((1, tm, 256), lambda b, i: (b, i, 0)),
            pl.BlockSpec((1, tm, 512), lambda b, i: (b, i, 0)),
            pl.BlockSpec((d, d), lambda b, i: (0, 0)),
            pl.BlockSpec((1, tm, d), lambda b, i: (b, i, 0)),
            pl.BlockSpec((1, mblk, 3 * d),
                         (lambda b, i: (b, 0, 0)) if mrows == 1 else (lambda b, i: (b, i, 0))),
            pl.BlockSpec((1, d), lambda b, i: (0, 0)),
        ],
        out_specs=pl.BlockSpec((1, tm, d), lambda b, i: (b, i, 0)),
        compiler_params=pltpu.CompilerParams(
            dimension_semantics=("parallel", "arbitrary"), vmem_limit_bytes=VMEM_LIMIT),
        name="output_projection",
    )(yf, ys, ym, w_out, x, mod_rows, g_post)


def _layer_weights(l, g_pre, g_post, w_in, b_f, g_q_a, w_uq, g_kv_a, w_uk, w_uv, w_out):
    o = IN_OFFSETS
    w = w_in[l]

    def cols(i):
        return w[:, o[i]:o[i + 1]]

    kpe = cols(11)
    half = MLA_ROPE // 2
    zeros = lambda n: jnp.zeros((D_MODEL, n), F32)
    grp = jnp.concatenate([kpe, cols(3), zeros(64 - MLA_ROPE - 4),
                           -kpe[:, half:], kpe[:, :half], zeros(64 - MLA_ROPE)], axis=1)
    w_main = jnp.concatenate([cols(0), cols(1), cols(2), cols(4), cols(5), cols(6), cols(7), cols(8),
                              cols(9), cols(10), grp, cols(12)], axis=1).astype(BF16)
    bf_row = jnp.zeros((1, LANES), F32).at[0, MLA_ROPE:MLA_ROPE + 4].set(b_f[l])

    uq = w_uq[l].reshape(Q_LORA, MLA_HEADS, MLA_NOPE + MLA_ROPE)
    w_nope = uq[:, :, :MLA_NOPE].reshape(Q_LORA, MLA_HEADS * MLA_NOPE).astype(BF16)
    x1 = uq[:, :, MLA_NOPE:MLA_NOPE + half]
    x2 = uq[:, :, MLA_NOPE + half:]
    zpad = jnp.zeros((Q_LORA, MLA_HEADS, LANES - MLA_ROPE), F32)
    w_xy = jnp.concatenate([x1, x2, zpad, -x2, x1, zpad], axis=2)
    w_xy = jnp.transpose(w_xy, (1, 0, 2)).astype(BF16)

    ukt = jnp.transpose(w_uk[l], (1, 2, 0))
    zk = jnp.zeros_like(ukt)
    even = jnp.concatenate([ukt, zk], axis=1)
    odd = jnp.concatenate([zk, ukt], axis=1)
    is_odd = (jnp.arange(MLA_HEADS) % 2 == 1)[:, None, None]
    w_k = jnp.where(is_odd, odd, even).astype(BF16)

    uvt = jnp.transpose(w_uv[l], (1, 0, 2))
    sel = (jnp.arange(MLA_HEADS)[:, None] == jnp.arange(MLA_HEADS)[None, :]).astype(F32)
    w_v = (uvt[:, :, None, :] * sel[:, None, :, None]).reshape(MLA_HEADS, KV_LORA, MLA_WIDTH)
    return dict(g_pre=g_pre[l][None], g_post=g_post[l][None], w_main=w_main, bf_row=bf_row,
                g_q=g_q_a[l][None], w_nope=w_nope, w_xy=w_xy, w_k=w_k, g_kv=g_kv_a[l][None],
                w_v=w_v.astype(BF16), w_out=w_out[l].astype(BF16))


def _rope_tables(pos):
    half = MLA_ROPE // 2
    inv = ROPE_THETA ** (-jnp.arange(half, dtype=F32) / half)
    ang = pos.astype(F32)[:, None] * inv[None, :]
    pad = jnp.zeros((pos.shape[0], LANES - MLA_ROPE), F32)
    cos = jnp.concatenate([jnp.cos(ang), jnp.cos(ang), pad], axis=1)
    sin = jnp.concatenate([jnp.sin(ang), jnp.sin(ang), pad], axis=1)
    return cos, sin


def _pad_rows(a, total):
    pad = total - a.shape[1]
    if pad == 0:
        return a
    return jnp.concatenate([a, jnp.zeros((a.shape[0], pad) + a.shape[2:], a.dtype)], axis=1)


def _forget_cumsum(logf_all, skv):
    nb, n, _ = logf_all.shape
    length = -(-max(n, skv) // 1024) * 1024
    rows = jnp.transpose(_pad_rows(logf_all, length), (0, 2, 1)).reshape(nb * N_FS_HEADS, length)
    frow = _cumsum_rows(rows).reshape(nb, N_FS_HEADS, length)[:, :, :skv]
    return frow, jnp.transpose(frow, (0, 2, 1))


def _stream_layer(x, mod_rows, cos, sin, lw, past_rows, *, tm, past, fs_tiles, mla_tiles, batch):
    (fq_b, fk_f, fk_b, fv_f, fv_b, fg_f, sq_b, sk_f, sk_b, sv_f, sv_b, sg_f,
     qcat_b, ckv_f, kcat_b, mg_f, gout_f, fv_t, sv_t, ckv_t) = _projection(x, mod_rows, cos, sin, lw, tm)
    nb, rows, _ = x.shape
    seq = nb * rows // batch

    def per_batch(a):
        return a.reshape((batch, seq) + a.shape[2:])

    fq_b, fk_f, fk_b, fv_f, fv_b, fg_f, sq_b, sk_f, sk_b, sv_f, sv_b, sg_f, ckv_f, kcat_b, mg_f, gout_f = [
        per_batch(a) for a in (fq_b, fk_f, fk_b, fv_f, fv_b, fg_f, sq_b, sk_f, sk_b, sv_f, sv_b, sg_f,
                               ckv_f, kcat_b, mg_f, gout_f)]
    qcat_b = jnp.transpose(qcat_b.reshape(nb, MLA_HEADS, batch // nb, seq, 256),
                           (0, 2, 1, 3, 4)).reshape(batch, MLA_HEADS, seq, 256)
    kpe_f = gout_f[:, :, :MLA_ROPE]
    logf = gout_f[:, :, MLA_ROPE:MLA_ROPE + N_FS_HEADS]
    new = (fk_f, fv_f, logf, sk_f, sv_f, ckv_f, kpe_f)

    n_kv = past + seq
    tq_fs, tk_fs = fs_tiles
    tq_m, tk_m = mla_tiles
    skv = -(-n_kv // max(tk_fs, tk_m)) * max(tk_fs, tk_m)
    if past_rows is None:
        fk_a, fv_a, sk_a, sv_a, kc_a, logf_a = fk_b, fv_b, sk_b, sv_b, kcat_b, logf
    else:
        c_fk, c_fv, c_logf, c_sk, c_sv, c_ckv, c_kpe = past_rows

        def join(c, n):
            c = c.reshape(batch, past, -1).astype(BF16)
            return _pad_rows(jnp.concatenate([c, n], axis=1), skv)

        fk_a, fv_a, sk_a, sv_a = join(c_fk, fk_b), join(c_fv, fv_b), join(c_sk, sk_b), join(c_sv, sv_b)
        c_kc = jnp.concatenate([c_ckv, c_kpe, jnp.ones((batch, past, 1), F32),
                                jnp.zeros((batch, past, LANES - MLA_ROPE - 1), F32)], axis=2)
        kc_a = join(c_kc, kcat_b)
        logf_a = jnp.concatenate([c_logf, logf], axis=1)
    frow, fcol = _forget_cumsum(logf_a, skv)

    if past_rows is None:
        y_fox = _fox_attention_t(fq_b, fk_a, fv_t, fcol, frow, fg_f, tq=tq_fs, tk=tk_fs)
        y_sb = _sb_attention_t(sq_b, sk_a, sv_t, sg_f, tq=tq_fs, tk=tk_fs)
        y_mla = _mla_attention_t(qcat_b, kc_a, ckv_t, lw["w_v"], mg_f, tq=tq_m, tk=tk_m)
    else:
        y_fox = _fox_attention(fq_b, fk_a, fv_a, fcol, frow, fg_f, past=past, tq=tq_fs, tk=tk_fs)
        y_sb = _sb_attention(sq_b, sk_a, sv_a, sg_f, past=past, tq=tq_fs, tk=tk_fs)
        y_mla = _mla_attention(qcat_b, kc_a, lw["w_v"], mg_f, past=past, n_kv=n_kv, tq=tq_m, tk=tk_m)

    def per_block(a):
        return a.reshape((nb, rows) + a.shape[2:])

    x_new = _output(per_block(y_fox), per_block(y_sb), per_block(y_mla), lw["w_out"], x, mod_rows,
                    lw["g_post"], tm)
    return x_new, new


def kernel(x_prompt, x_sample, c_prompt, c_sample, cache_fox_k, cache_fox_v, cache_fox_logf, cache_sb_k, cache_sb_v, cache_mla_ckv, cache_mla_kpe, g_pre, g_post, w_ada, b_ada, w_in, b_f, g_q_a, w_uq, g_kv_a, w_uk, w_uv, w_out):
    batch, seq, d = x_prompt.shape
    dec_batch, dec_seq, _ = x_sample.shape
    past_len = cache_fox_k.shape[2]
    dec_rows = dec_batch * dec_seq

    mod = _modulation(jnp.concatenate([c_prompt, c_sample], axis=0), w_ada, b_ada)
    cos_p, sin_p = _rope_tables(jnp.arange(seq, dtype=jnp.int32))
    pos_s = past_len + (jnp.arange(dec_rows, dtype=jnp.int32) % dec_seq)
    cos_s, sin_s = _rope_tables(pos_s)
    skv_s = -(-(past_len + dec_seq) // SB_SUB) * SB_SUB

    y_p = x_prompt
    y_s = x_sample.reshape(1, dec_rows, d)
    rows_p, rows_s = [], []
    for l in range(DEPTH):
        lw = _layer_weights(l, g_pre, g_post, w_in, b_f, g_q_a, w_uq, g_kv_a, w_uk, w_uv, w_out)
        mod_p = mod[l, :batch][:, None, :]
        mod_s = jnp.repeat(mod[l, batch:], dec_seq, axis=0)[None]
        y_p, new_p = _stream_layer(y_p, mod_p, cos_p, sin_p, lw, None, tm=512, past=0,
                                   fs_tiles=(256, 512), mla_tiles=(128, 512), batch=batch)
        past_rows = (cache_fox_k[l], cache_fox_v[l], cache_fox_logf[l], cache_sb_k[l], cache_sb_v[l],
                     cache_mla_ckv[l], cache_mla_kpe[l])
        y_s, new_s = _stream_layer(y_s, mod_s, cos_s, sin_s, lw, past_rows, tm=dec_rows, past=past_len,
                                   fs_tiles=(dec_seq, skv_s), mla_tiles=(dec_seq, skv_s),
                                   batch=dec_batch)
        rows_p.append(new_p)
        rows_s.append(new_s)

    def stack(rows, idx, shape_tail):
        a = jnp.stack([r[idx] for r in rows])
        return a.reshape(a.shape[:3] + shape_tail)

    heads = (N_FS_HEADS, HEAD_DIM)
    tails = (heads, heads, (N_FS_HEADS,), heads, heads, (KV_LORA,), (MLA_ROPE,))
    outs_p = [stack(rows_p, i, t) for i, t in enumerate(tails)]
    outs_s = [stack(rows_s, i, t) for i, t in enumerate(tails)]
    return (y_p, y_s.reshape(dec_batch, dec_seq, d), *outs_p, *outs_s)
```

```python
import functools

import numpy as np
import jax
import jax.numpy as jnp
from jax import lax
from jax.experimental import pallas as pl
from jax.experimental.pallas import tpu as pltpu

D_MODEL = 1024
DEPTH = 2
CHUNK_SHIFT = 6
HEAD_DIM = 64
N_FS_HEADS = 4
FS_WIDTH = N_FS_HEADS * HEAD_DIM
MLA_HEADS = 8
MLA_GROUP = 2
MLA_NOPE = 64
MLA_ROPE = 32
MLA_V = 64
MLA_WIDTH = MLA_HEADS * MLA_V
Q_LORA = 256
KV_LORA = 128
ROPE_THETA = 10000.0
EPS = 1e-6
IN_SPLITS = (256, 256, 256, 4, 256, 256, 256, 256, 256, Q_LORA, KV_LORA, MLA_ROPE, MLA_WIDTH)
IN_OFFSETS = tuple(int(v) for v in np.cumsum((0,) + IN_SPLITS))

LANES = 128
SB_SUB = 256
CUMSUM_ROWS = 8
W_MAIN = 3072
NEG = -1e30
SB_DEAD = -110.0
FOX_DEAD = -110.0
NORM_SLACK = 1.0 + 2.0 ** -7
FS_SCALE = HEAD_DIM ** -0.5
MLA_SCALE = (MLA_NOPE + MLA_ROPE) ** -0.5
MLA_EXP2_SCALE = MLA_SCALE * float(np.log2(np.e))
ONES_LANE = KV_LORA + MLA_ROPE
ONES_ROWS = 16
VMEM_LIMIT = 56 * 1024 * 1024

BF16 = jnp.bfloat16
F32 = jnp.float32


def _dot(a, b):
    return jnp.dot(a, b, preferred_element_type=F32)


def _dot_nt(a, b):
    return lax.dot_general(a, b, (((1,), (1,)), ((), ())), preferred_element_type=F32)


def _silu(g):
    return g / (1.0 + jnp.exp(-g))


def _rms(x, g):
    return x * lax.rsqrt(jnp.mean(x * x, axis=-1, keepdims=True) + EPS) * g


def _mod_kernel(c_ref, w_ref, b_ref, o_ref):
    a = _silu(c_ref[...]).astype(BF16)
    o_ref[0] = _dot(a, w_ref[0].astype(BF16)) + b_ref[0]


def _modulation(c_all, w_ada, b_ada):
    n = c_all.shape[0]
    d = D_MODEL
    return pl.pallas_call(
        _mod_kernel,
        out_shape=jax.ShapeDtypeStruct((DEPTH, n, 3 * d), F32),
        grid=(DEPTH, 3),
        in_specs=[
            pl.BlockSpec((n, d), lambda l, j: (0, 0)),
            pl.BlockSpec((1, d, d), lambda l, j: (l, 0, j)),
            pl.BlockSpec((1, 1, d), lambda l, j: (l, 0, j)),
        ],
        out_specs=pl.BlockSpec((1, n, d), lambda l, j: (l, 0, j)),
        compiler_params=pltpu.CompilerParams(
            dimension_semantics=("arbitrary", "arbitrary"), vmem_limit_bytes=VMEM_LIMIT),
        name="modulation",
    )(c_all, w_ada, b_ada.reshape(DEPTH, 1, 3 * d))


def _proj_kernel(x_ref, mod_ref, gpre_ref, w_ref, bf_ref, gq_ref, wn_ref, wxy_ref, wk_ref,
                 gkv_ref, cos_ref, sin_ref,
                 fq_b, fk_f, fk_b, fv_f, fv_b, fg_f, sq_b, sk_f, sk_b, sv_f, sv_b, sg_f,
                 qcat_b, ckv_f, kcat_b, mg_f, gout_f, fv_t, sv_t, ckv_t):
    d = D_MODEL
    x = x_ref[0]
    mod = mod_ref[0]
    shift = mod[:, 0:d]
    scale = mod[:, d:2 * d]
    h = _rms(x, gpre_ref[...]) * (1.0 + scale) + shift
    hb = h.astype(BF16)

    def proj(a, b):
        return _dot(hb, w_ref[:, a:b])

    fq_b[0] = (proj(0, 256) * FS_SCALE).astype(BF16)
    t = proj(256, 512)
    fk_f[0] = t
    fk_b[0] = t.astype(BF16)
    t = proj(512, 768)
    fv_f[0] = t
    fv_b[0] = t.astype(BF16)
    fv_t[0] = t.T.astype(BF16)
    fg_f[0] = proj(768, 1024)
    sq_b[0] = (proj(1024, 1280) * FS_SCALE).astype(BF16)
    t = proj(1280, 1536)
    sk_f[0] = t
    sk_b[0] = t.astype(BF16)
    t = proj(1536, 1792)
    sv_f[0] = t
    sv_b[0] = t.astype(BF16)
    sv_t[0] = t.T.astype(BF16)
    sg_f[0] = proj(1792, 2048)
    mg_f[0] = proj(2560, 3072)

    cos = cos_ref[...]
    sin = sin_ref[...]

    cg = proj(2304, 2560)
    ckvn = _rms(cg[:, 0:KV_LORA], gkv_ref[...])
    ckv_f[0] = ckvn
    ckv_t[0] = jnp.concatenate([ckvn.T, jnp.ones((ONES_ROWS, ckvn.shape[0]), F32)], axis=0).astype(BF16)
    grp = cg[:, KV_LORA:2 * KV_LORA]
    rope_k = grp * cos + pltpu.roll(grp, 64, axis=1) * sin
    zf = grp + bf_ref[...]
    logf = jnp.minimum(zf, 0.0) - jnp.log(1.0 + jnp.exp(-jnp.abs(zf)))
    lane = lax.broadcasted_iota(jnp.int32, grp.shape, 1)
    gout_f[0] = jnp.where(lane < MLA_ROPE, rope_k, logf)
    ones_lane = jnp.where(lane == ONES_LANE - KV_LORA, 1.0, 0.0)
    kcat_b[0] = jnp.concatenate([ckvn, rope_k + ones_lane], axis=1).astype(BF16)

    cqn = _rms(proj(2048, 2304), gq_ref[...]).astype(BF16)
    qn = _dot(cqn, wn_ref[...]).astype(BF16)
    for hh in range(MLA_HEADS):
        pair = qn[:, (hh // 2) * LANES:(hh // 2 + 1) * LANES]
        qlat = _dot(pair, wk_ref[hh])
        xy = _dot(cqn, wxy_ref[hh])
        rope_q = xy[:, 0:LANES] * cos + xy[:, LANES:2 * LANES] * sin
        qcat_b[0, hh] = (jnp.concatenate([qlat, rope_q], axis=1) * MLA_EXP2_SCALE).astype(BF16)


def _projection(x, mod_rows, cos, sin, lw, tm):
    nb, s, d = x.shape
    mrows = mod_rows.shape[1]
    mblk = 1 if mrows == 1 else tm
    grid = (nb, s // tm)

    def row(width, dtype):
        return (jax.ShapeDtypeStruct((nb, s, width), dtype),
                pl.BlockSpec((1, tm, width), lambda b, i: (b, i, 0)))

    def col(width):
        return (jax.ShapeDtypeStruct((nb, width, s), BF16),
                pl.BlockSpec((1, width, tm), lambda b, i: (b, 0, i)))

    outs = [row(256, BF16), row(256, F32), row(256, BF16), row(256, F32), row(256, BF16),
            row(256, F32), row(256, BF16), row(256, F32), row(256, BF16), row(256, F32),
            row(256, BF16), row(256, F32),
            (jax.ShapeDtypeStruct((nb, MLA_HEADS, s, 256), BF16),
             pl.BlockSpec((1, MLA_HEADS, tm, 256), lambda b, i: (b, 0, i, 0))),
            row(KV_LORA, F32), row(256, BF16), row(MLA_WIDTH, F32), row(LANES, F32),
            col(256), col(256), col(KV_LORA + ONES_ROWS)]

    def full(a):
        nd = a.ndim
        return pl.BlockSpec(a.shape, lambda b, i: (0,) * nd)

    consts = [lw["g_pre"], lw["w_main"], lw["bf_row"], lw["g_q"], lw["w_nope"], lw["w_xy"],
              lw["w_k"], lw["g_kv"]]
    in_specs = ([pl.BlockSpec((1, tm, d), lambda b, i: (b, i, 0)),
                 pl.BlockSpec((1, mblk, 3 * d),
                              (lambda b, i: (b, 0, 0)) if mrows == 1 else (lambda b, i: (b, i, 0)))]
                + [full(a) for a in consts]
                + [pl.BlockSpec((tm, LANES), lambda b, i: (i, 0)),
                   pl.BlockSpec((tm, LANES), lambda b, i: (i, 0))])
    return pl.pallas_call(
        _proj_kernel,
        out_shape=[o[0] for o in outs],
        grid=grid,
        in_specs=in_specs,
        out_specs=[o[1] for o in outs],
        compiler_params=pltpu.CompilerParams(
            dimension_semantics=("parallel", "arbitrary"), vmem_limit_bytes=VMEM_LIMIT),
        name="projection",
    )(x, mod_rows, *consts, cos, sin)


def _cumsum_kernel(x_ref, u_ref, l_ref, o_ref):
    g, n, _ = x_ref.shape
    w_all = jnp.dot(x_ref[...].reshape(g * n, LANES), u_ref[...], preferred_element_type=F32,
                    precision=lax.Precision.HIGHEST)
    for i in range(g):
        w = w_all[i * n:(i + 1) * n]
        tot = jnp.broadcast_to(w[:, LANES - 1:LANES], w.shape)
        off = jnp.dot(l_ref[...], tot, preferred_element_type=F32, precision=lax.Precision.HIGHEST)
        o_ref[i] = w + off


def _cumsum_rows(x):
    r, length = x.shape
    n = length // LANES
    u = jnp.asarray(np.triu(np.ones((LANES, LANES), np.float32)))
    lo = jnp.asarray(np.tril(np.ones((n, n), np.float32), -1))
    out = pl.pallas_call(
        _cumsum_kernel,
        out_shape=jax.ShapeDtypeStruct((r, n, LANES), F32),
        grid=(r // CUMSUM_ROWS,),
        in_specs=[pl.BlockSpec((CUMSUM_ROWS, n, LANES), lambda i: (i, 0, 0)),
                  pl.BlockSpec((LANES, LANES), lambda i: (0, 0)),
                  pl.BlockSpec((n, n), lambda i: (0, 0))],
        out_specs=pl.BlockSpec((CUMSUM_ROWS, n, LANES), lambda i: (i, 0, 0)),
        compiler_params=pltpu.CompilerParams(dimension_semantics=("arbitrary",)),
        name="cumsum_logf",
    )(x.reshape(r, n, LANES), u, lo)
    return out.reshape(r, length)


def _schedule(nq, kmax_fn, reverse):
    qi_l, kj_l, first_l, last_l = [], [], [], []
    for qi in range(nq):
        ks = list(range(kmax_fn(qi) + 1))
        if reverse:
            ks = ks[::-1]
        for n, kj in enumerate(ks):
            qi_l.append(qi)
            kj_l.append(kj)
            first_l.append(int(n == 0))
            last_l.append(int(n == len(ks) - 1))
    return tuple(jnp.asarray(np.asarray(a, np.int32)) for a in (qi_l, kj_l, first_l, last_l))


def _head_lane_mask(shape, h):
    lane = lax.broadcasted_iota(jnp.int32, shape, 1)
    return (lane >= HEAD_DIM * h) & (lane < HEAD_DIM * (h + 1))


def _fox_kernel(qt, kt, ft, lt, q_ref, k_ref, v_ref, fc_ref, fr_ref, g_ref, o_ref,
                qm_sc, m_sc, l_sc, acc_sc, *, past, tq, tk):
    step = pl.program_id(1)
    qi = qt[step]
    kj = kt[step]

    @pl.when(ft[step] == 1)
    def _():
        q = q_ref[0]
        for h in range(N_FS_HEADS):
            qm_sc[h] = jnp.where(_head_lane_mask(q.shape, h), q, jnp.zeros_like(q))
        m_sc[...] = jnp.full(m_sc.shape, NEG, F32)
        l_sc[...] = jnp.zeros(l_sc.shape, F32)
        acc_sc[...] = jnp.zeros(acc_sc.shape, F32)

    k = k_ref[0]
    v = v_ref[0]
    q_pos = past + qi * tq + lax.broadcasted_iota(jnp.int32, (tq, 1), 0)
    k_pos = kj * tk + lax.broadcasted_iota(jnp.int32, (1, tk), 1)
    mask = k_pos <= q_pos
    fc = fc_ref[0]
    fr = fr_ref[0]
    for h in range(N_FS_HEADS):
        s = _dot_nt(qm_sc[h], k)
        s = s + (fc[:, h:h + 1] - fr[h:h + 1, :])
        s = jnp.where(mask, s, NEG)
        m_prev = m_sc[h]
        m_new = jnp.maximum(m_prev, jnp.max(s, axis=1, keepdims=True))
        p = jnp.exp(s - m_new)
        alpha = jnp.exp(m_prev - m_new)
        l_sc[h] = alpha * l_sc[h] + jnp.sum(p, axis=1, keepdims=True)
        acc_sc[h] = alpha * acc_sc[h] + _dot(p.astype(BF16), v)
        m_sc[h] = m_new

    @pl.when(lt[step] == 1)
    def _():
        o = jnp.zeros((tq, FS_WIDTH), F32)
        for h in range(N_FS_HEADS):
            o = jnp.where(_head_lane_mask(o.shape, h), acc_sc[h] / l_sc[h], o)
        o_ref[0] = (o * _silu(g_ref[0])).astype(BF16)


def _fox_attention(q, k, v, fcol, frow, g, *, past, tq, tk):
    nb, sq, _ = q.shape
    nq = sq // tq
    tabs = _schedule(nq, lambda qi: (past + (qi + 1) * tq - 1) // tk, False)
    nsteps = int(tabs[0].shape[0])
    qoff = past // tq
    kern = functools.partial(_fox_kernel, past=past, tq=tq, tk=tk)
    return pl.pallas_call(
        kern,
        out_shape=jax.ShapeDtypeStruct((nb, sq, FS_WIDTH), BF16),
        grid_spec=pltpu.PrefetchScalarGridSpec(
            num_scalar_prefetch=4,
            grid=(nb, nsteps),
            in_specs=[
                pl.BlockSpec((1, tq, FS_WIDTH), lambda b, s, qt, kt, ft, lt: (b, qt[s], 0)),
                pl.BlockSpec((1, tk, FS_WIDTH), lambda b, s, qt, kt, ft, lt: (b, kt[s], 0)),
                pl.BlockSpec((1, tk, FS_WIDTH), lambda b, s, qt, kt, ft, lt: (b, kt[s], 0)),
                pl.BlockSpec((1, tq, N_FS_HEADS), lambda b, s, qt, kt, ft, lt: (b, qoff + qt[s], 0)),
                pl.BlockSpec((1, N_FS_HEADS, tk), lambda b, s, qt, kt, ft, lt: (b, 0, kt[s])),
                pl.BlockSpec((1, tq, FS_WIDTH), lambda b, s, qt, kt, ft, lt: (b, qt[s], 0)),
            ],
            out_specs=pl.BlockSpec((1, tq, FS_WIDTH), lambda b, s, qt, kt, ft, lt: (b, qt[s], 0)),
            scratch_shapes=[
                pltpu.VMEM((N_FS_HEADS, tq, FS_WIDTH), BF16),
                pltpu.VMEM((N_FS_HEADS, tq, 1), F32),
                pltpu.VMEM((N_FS_HEADS, tq, 1), F32),
                pltpu.VMEM((N_FS_HEADS, tq, FS_WIDTH), F32),
            ],
        ),
        compiler_params=pltpu.CompilerParams(
            dimension_semantics=("parallel", "arbitrary"), vmem_limit_bytes=VMEM_LIMIT),
        name="fox_attention",
    )(*tabs, q, k, v, fcol, frow, g)


def _sb_kernel(qt, kt, ft, lt, q_ref, k_ref, v_ref, tri_ref, g_ref, o_ref,
               qm_sc, r_sc, acc_sc, *, past, tq, tk):
    step = pl.program_id(1)
    qi = qt[step]
    kj = kt[step]

    @pl.when(ft[step] == 1)
    def _():
        q = q_ref[0]
        for h in range(N_FS_HEADS):
            qm_sc[h] = jnp.where(_head_lane_mask(q.shape, h), q, jnp.zeros_like(q))
        r_sc[...] = jnp.zeros(r_sc.shape, F32)
        acc_sc[...] = jnp.zeros(acc_sc.shape, F32)

    q_pos = past + qi * tq + lax.broadcasted_iota(jnp.int32, (tq, 1), 0)

    def sub_block(sub):
        tri = tri_ref[...]
        k = k_ref[0, sub * SB_SUB:(sub + 1) * SB_SUB, :]
        v = v_ref[0, sub * SB_SUB:(sub + 1) * SB_SUB, :]
        k_pos = kj * tk + sub * SB_SUB + lax.broadcasted_iota(jnp.int32, (1, SB_SUB), 1)
        mask = k_pos < q_pos
        for h in range(N_FS_HEADS):
            z = _dot_nt(qm_sc[h], k)
            sp = jnp.maximum(z, 0.0) + jnp.log(1.0 + jnp.exp(-jnp.abs(z)))
            l1m = jnp.where(mask, -sp, 0.0)
            hi = l1m.astype(BF16)
            lo = (l1m - hi.astype(F32)).astype(BF16)
            c = _dot(hi, tri) + _dot(lo, tri)
            r = r_sc[h]
            a = jnp.where(mask, jnp.exp(z + (c + r)), 0.0)
            acc_sc[h] = acc_sc[h] + _dot(a.astype(BF16), v)
            r_sc[h] = r + c[:, 0:1]

    for sub in reversed(range(tk // SB_SUB)):
        pl.when(jnp.max(r_sc[...]) > SB_DEAD)(functools.partial(sub_block, sub))

    @pl.when(lt[step] == 1)
    def _():
        o = jnp.zeros((tq, FS_WIDTH), F32)
        for h in range(N_FS_HEADS):
            o = jnp.where(_head_lane_mask(o.shape, h), acc_sc[h], o)
        o_ref[0] = (o * _silu(g_ref[0])).astype(BF16)


def _sb_attention(q, k, v, g, *, past, tq, tk):
    nb, sq, _ = q.shape
    nq = sq // tq
    tabs = _schedule(nq, lambda qi: (past + (qi + 1) * tq - 1) // tk, True)
    nsteps = int(tabs[0].shape[0])
    tri = jnp.asarray(np.tril(np.ones((SB_SUB, SB_SUB), np.float32))).astype(BF16)
    kern = functools.partial(_sb_kernel, past=past, tq=tq, tk=tk)
    return pl.pallas_call(
        kern,
        out_shape=jax.ShapeDtypeStruct((nb, sq, FS_WIDTH), BF16),
        grid_spec=pltpu.PrefetchScalarGridSpec(
            num_scalar_prefetch=4,
            grid=(nb, nsteps),
            in_specs=[
                pl.BlockSpec((1, tq, FS_WIDTH), lambda b, s, qt, kt, ft, lt: (b, qt[s], 0)),
                pl.BlockSpec((1, tk, FS_WIDTH), lambda b, s, qt, kt, ft, lt: (b, kt[s], 0)),
                pl.BlockSpec((1, tk, FS_WIDTH), lambda b, s, qt, kt, ft, lt: (b, kt[s], 0)),
                pl.BlockSpec((SB_SUB, SB_SUB), lambda b, s, qt, kt, ft, lt: (0, 0)),
                pl.BlockSpec((1, tq, FS_WIDTH), lambda b, s, qt, kt, ft, lt: (b, qt[s], 0)),
            ],
            out_specs=pl.BlockSpec((1, tq, FS_WIDTH), lambda b, s, qt, kt, ft, lt: (b, qt[s], 0)),
            scratch_shapes=[
                pltpu.VMEM((N_FS_HEADS, tq, FS_WIDTH), BF16),
                pltpu.VMEM((N_FS_HEADS, tq, 1), F32),
                pltpu.VMEM((N_FS_HEADS, tq, FS_WIDTH), F32),
            ],
        ),
        compiler_params=pltpu.CompilerParams(
            dimension_semantics=("parallel", "arbitrary"), vmem_limit_bytes=VMEM_LIMIT),
        name="sb_attention",
    )(*tabs, q, k, v, tri, g)


def _mla_kernel(qt, kt, ft, lt, q_ref, k_ref, wv_ref, g_ref, o_ref,
                m_sc, acc_sc, *, past, n_kv, tq, tk):
    step = pl.program_id(1)
    qi = qt[step]
    kj = kt[step]
    rows = MLA_HEADS * tq

    @pl.when(ft[step] == 1)
    def _():
        m_sc[...] = jnp.full(m_sc.shape, NEG, F32)
        acc_sc[...] = jnp.zeros(acc_sc.shape, F32)

    k = k_ref[0]
    q_pos = past + qi * tq + lax.broadcasted_iota(jnp.int32, (tq, 1), 0)
    k_pos = kj * tk + lax.broadcasted_iota(jnp.int32, (1, tk), 1)
    mask = ((k_pos >> CHUNK_SHIFT) <= (q_pos >> CHUNK_SHIFT)) & (k_pos < n_kv)
    s = _dot_nt(q_ref[0].reshape(rows, 256), k).reshape(MLA_HEADS, tq, tk)
    s = jnp.where(mask[None], s, NEG)
    m_prev = m_sc[...]
    m_new = jnp.maximum(m_prev, jnp.max(s, axis=2, keepdims=True))
    p = jnp.exp2(s - m_new)
    alpha = jnp.exp2(m_prev - m_new)
    pv = _dot(p.reshape(rows, tk).astype(BF16), k)
    acc_sc[...] = alpha.reshape(rows, 1) * acc_sc[...] + pv
    m_sc[...] = m_new

    @pl.when(lt[step] == 1)
    def _():
        acc = acc_sc[...]
        o_lat = (acc[:, 0:KV_LORA] / acc[:, ONES_LANE:ONES_LANE + 1]).astype(BF16)
        o = jnp.zeros((tq, MLA_WIDTH), F32)
        for h in range(MLA_HEADS):
            o = o + _dot(o_lat[h * tq:(h + 1) * tq], wv_ref[h])
        o_ref[0] = (o * _silu(g_ref[0])).astype(BF16)


def _mla_attention(qcat, kcat, wv, g, *, past, n_kv, tq, tk):
    nb, _, sq, _ = qcat.shape
    nq = sq // tq

    def kmax(qi):
        last_q = past + (qi + 1) * tq - 1
        end = min(((last_q >> CHUNK_SHIFT) + 1) << CHUNK_SHIFT, n_kv)
        return (end - 1) // tk

    tabs = _schedule(nq, kmax, False)
    nsteps = int(tabs[0].shape[0])
    kern = functools.partial(_mla_kernel, past=past, n_kv=n_kv, tq=tq, tk=tk)
    return pl.pallas_call(
        kern,
        out_shape=jax.ShapeDtypeStruct((nb, sq, MLA_WIDTH), BF16),
        grid_spec=pltpu.PrefetchScalarGridSpec(
            num_scalar_prefetch=4,
            grid=(nb, nsteps),
            in_specs=[
                pl.BlockSpec((1, MLA_HEADS, tq, 256), lambda b, s, qt, kt, ft, lt: (b, 0, qt[s], 0)),
                pl.BlockSpec((1, tk, 256), lambda b, s, qt, kt, ft, lt: (b, kt[s], 0)),
                pl.BlockSpec((MLA_HEADS, KV_LORA, MLA_WIDTH), lambda b, s, qt, kt, ft, lt: (0, 0, 0)),
                pl.BlockSpec((1, tq, MLA_WIDTH), lambda b, s, qt, kt, ft, lt: (b, qt[s], 0)),
            ],
            out_specs=pl.BlockSpec((1, tq, MLA_WIDTH), lambda b, s, qt, kt, ft, lt: (b, qt[s], 0)),
            scratch_shapes=[
                pltpu.VMEM((MLA_HEADS, tq, 1), F32),
                pltpu.VMEM((MLA_HEADS * tq, 256), F32),
            ],
        ),
        compiler_params=pltpu.CompilerParams(
            dimension_semantics=("parallel", "arbitrary"), vmem_limit_bytes=VMEM_LIMIT),
        name="mla_attention",
    )(*tabs, qcat, kcat, wv, g)


def _stack_heads(q, qm_sc, tq):
    for h in range(N_FS_HEADS):
        qm_sc[h * tq:(h + 1) * tq, :] = jnp.where(_head_lane_mask(q.shape, h), q, jnp.zeros_like(q))


def _query_positions(past, qi, tq, heads):
    lane = lax.broadcasted_iota(jnp.int32, (1, heads * tq), 1)
    return past + qi * tq + (lane & (tq - 1))


def _log2(n):
    assert n & (n - 1) == 0
    return n.bit_length() - 1


def _live_flag(j, value, floor):
    return jnp.logical_and(j >= 0, value > floor).astype(jnp.int32)


def _fox_t_kernel(q_ref, k_ref, vt_ref, fc_ref, fr_ref, g_ref, e_ref, o_ref,
                  qm_sc, fkb_sc, pm_sc, fq_sc, qn_sc, m_sc, l_sc, acc_sc, *, tq, tk):
    qi = pl.program_id(1)
    seq = k_ref.shape[1]
    groups = tq // LANES

    def per_head_row(x):
        return jnp.concatenate([x[:, (g // groups) * LANES:(g // groups + 1) * LANES]
                                for g in range(N_FS_HEADS * groups)], axis=1)

    @pl.when(qi == 0)
    def _():
        fc = fc_ref[0]
        for h in range(N_FS_HEADS):
            fkb_sc[h] = jnp.broadcast_to(fc[:, h:h + 1], (seq, LANES))
        running = jnp.zeros((1, N_FS_HEADS * LANES), F32)
        for j in range(seq // tk):
            kf = k_ref[0, j * tk:(j + 1) * tk, :].astype(F32)
            n2 = _dot((kf * kf).astype(BF16), e_ref[...])
            running = jnp.maximum(running, jnp.max(n2, axis=0, keepdims=True))
            pm_sc[j:j + 1, :] = running

    _stack_heads(q_ref[0], qm_sc, tq)
    fr = fr_ref[0]
    fq_sc[...] = jnp.concatenate([fr[h:h + 1, :] for h in range(N_FS_HEADS)], axis=1)
    qf = qm_sc[...].astype(F32)
    qn_sc[...] = _dot_nt(jnp.ones((8, FS_WIDTH), BF16), (qf * qf).astype(BF16))[0:1, :]
    m_sc[...] = jnp.full(m_sc.shape, NEG, F32)
    l_sc[...] = jnp.zeros(l_sc.shape, F32)
    acc_sc[...] = jnp.zeros(acc_sc.shape, F32)

    def tile(j, masked):
        start = pl.multiple_of(j * tk, tk)
        s_t = _dot_nt(k_ref[0, pl.ds(start, tk), :], qm_sc[...])
        fk = fkb_sc[:, pl.ds(start, tk), :]
        s_t = jnp.concatenate(
            [s_t[:, g * LANES:(g + 1) * LANES] - fk[g // groups] for g in range(N_FS_HEADS * groups)],
            axis=1) + fq_sc[...]
        if masked:
            k_pos = j * tk + lax.broadcasted_iota(jnp.int32, (tk, 1), 0)
            s_t = jnp.where(k_pos <= _query_positions(0, qi, tq, N_FS_HEADS), s_t, NEG)
        m_prev = m_sc[...]
        m_new = jnp.maximum(m_prev, jnp.max(s_t, axis=0, keepdims=True))
        p_t = jnp.exp(s_t - m_new)
        alpha = jnp.exp(m_prev - m_new)
        l_sc[...] = alpha * l_sc[...] + jnp.sum(p_t, axis=0, keepdims=True)
        m_sc[...] = m_new
        p_b = p_t.astype(BF16)
        for h in range(N_FS_HEADS):
            pv = _dot(vt_ref[0, h * HEAD_DIM:(h + 1) * HEAD_DIM, pl.ds(start, tk)],
                      p_b[:, h * tq:(h + 1) * tq])
            acc_sc[h] = alpha[:, h * tq:(h + 1) * tq] * acc_sc[h] + pv

    def live(j):
        jc = jnp.maximum(j, 0)
        pm = per_head_row(pm_sc[pl.ds(jc, 1), :])
        fend = per_head_row(jnp.concatenate(
            [fkb_sc[h, pl.ds((jc + 1) * tk - 1, 1), :] for h in range(N_FS_HEADS)], axis=1))
        bound = jnp.sqrt(qn_sc[...] * pm) * NORM_SLACK + fq_sc[...] - fend - m_sc[...]
        return _live_flag(j, jnp.max(bound), FOX_DEAD)

    j_diag = ((qi + 1) * tq - 1) >> _log2(tk)
    tile(j_diag, True)

    def step(carry):
        j = carry[0]
        tile(j, False)
        return j - 1, live(j - 1)

    lax.while_loop(lambda c: c[1] > 0, step, (j_diag - 1, live(j_diag - 1)))

    l = l_sc[...]
    o_t = jnp.concatenate([acc_sc[h] / l[:, h * tq:(h + 1) * tq] for h in range(N_FS_HEADS)], axis=0)
    o_ref[0] = (o_t.T * _silu(g_ref[0])).astype(BF16)


def _fox_attention_t(q, k, vt, fcol, frow, g, *, tq, tk):
    nb, sq, _ = q.shape
    rows = N_FS_HEADS * tq
    head_of_col = np.arange(FS_WIDTH)[:, None] // HEAD_DIM
    head_of_lane = np.arange(N_FS_HEADS * LANES)[None, :] // LANES
    expand = jnp.asarray((head_of_col == head_of_lane).astype(np.float32)).astype(BF16)
    kern = functools.partial(_fox_t_kernel, tq=tq, tk=tk)
    return pl.pallas_call(
        kern,
        out_shape=jax.ShapeDtypeStruct((nb, sq, FS_WIDTH), BF16),
        grid=(nb, sq // tq),
        in_specs=[
            pl.BlockSpec((1, tq, FS_WIDTH), lambda b, i: (b, i, 0)),
            pl.BlockSpec((1, sq, FS_WIDTH), lambda b, i: (b, 0, 0)),
            pl.BlockSpec((1, FS_WIDTH, sq), lambda b, i: (b, 0, 0)),
            pl.BlockSpec((1, sq, N_FS_HEADS), lambda b, i: (b, 0, 0)),
            pl.BlockSpec((1, N_FS_HEADS, tq), lambda b, i: (b, 0, i)),
            pl.BlockSpec((1, tq, FS_WIDTH), lambda b, i: (b, i, 0)),
            pl.BlockSpec(expand.shape, lambda b, i: (0, 0)),
        ],
        out_specs=pl.BlockSpec((1, tq, FS_WIDTH), lambda b, i: (b, i, 0)),
        scratch_shapes=[
            pltpu.VMEM((rows, FS_WIDTH), BF16),
            pltpu.VMEM((N_FS_HEADS, sq, LANES), F32),
            pltpu.VMEM((sq // tk, N_FS_HEADS * LANES), F32),
            pltpu.VMEM((1, rows), F32),
            pltpu.VMEM((1, rows), F32),
            pltpu.VMEM((1, rows), F32),
            pltpu.VMEM((1, rows), F32),
            pltpu.VMEM((N_FS_HEADS, HEAD_DIM, tq), F32),
        ],
        compiler_params=pltpu.CompilerParams(
            dimension_semantics=("parallel", "arbitrary"), vmem_limit_bytes=VMEM_LIMIT),
        name="fox_attention_t",
    )(q, k, vt, fcol, frow, g, expand)


def _sb_t_kernel(q_ref, k_ref, vt_ref, tri_ref, g_ref, o_ref, qm_sc, r_sc, acc_sc, *, tq, tk):
    qi = pl.program_id(1)
    _stack_heads(q_ref[0], qm_sc, tq)
    r_sc[...] = jnp.zeros(r_sc.shape, F32)
    acc_sc[...] = jnp.zeros(acc_sc.shape, F32)

    def tile(j, masked):
        start = pl.multiple_of(j * tk, tk)
        z_all = _dot_nt(k_ref[0, pl.ds(start, tk), :], qm_sc[...])
        tri_t = tri_ref[...]
        for sub in reversed(range(tk // SB_SUB)):
            z = z_all[sub * SB_SUB:(sub + 1) * SB_SUB, :]
            l1m = -(jnp.maximum(z, 0.0) + jnp.log(1.0 + jnp.exp(-jnp.abs(z))))
            if masked:
                k_pos = j * tk + sub * SB_SUB + lax.broadcasted_iota(jnp.int32, (SB_SUB, 1), 0)
                mask = k_pos < _query_positions(0, qi, tq, N_FS_HEADS)
                l1m = jnp.where(mask, l1m, 0.0)
            hi = l1m.astype(BF16)
            lo = (l1m - hi.astype(F32)).astype(BF16)
            c = _dot(tri_t, hi) + _dot(tri_t, lo)
            r = r_sc[...]
            a = jnp.exp(z + (c + r))
            if masked:
                a = jnp.where(mask, a, 0.0)
            a_b = a.astype(BF16)
            for h in range(N_FS_HEADS):
                acc_sc[h] = acc_sc[h] + _dot(
                    vt_ref[0, h * HEAD_DIM:(h + 1) * HEAD_DIM,
                           pl.ds(pl.multiple_of(start + sub * SB_SUB, SB_SUB), SB_SUB)],
                    a_b[:, h * tq:(h + 1) * tq])
            r_sc[...] = r + c[0:1, :]

    def live(j):
        return _live_flag(j, jnp.max(r_sc[...]), SB_DEAD)

    j_diag = ((qi + 1) * tq - 1) >> _log2(tk)
    tile(j_diag, True)

    def step(carry):
        j = carry[0]
        tile(j, False)
        return j - 1, live(j - 1)

    lax.while_loop(lambda c: c[1] > 0, step, (j_diag - 1, live(j_diag - 1)))

    o_t = jnp.concatenate([acc_sc[h] for h in range(N_FS_HEADS)], axis=0)
    o_ref[0] = (o_t.T * _silu(g_ref[0])).astype(BF16)


def _sb_attention_t(q, k, vt, g, *, tq, tk):
    nb, sq, _ = q.shape
    rows = N_FS_HEADS * tq
    tri_t = jnp.asarray(np.triu(np.ones((SB_SUB, SB_SUB), np.float32))).astype(BF16)
    kern = functools.partial(_sb_t_kernel, tq=tq, tk=tk)
    return pl.pallas_call(
        kern,
        out_shape=jax.ShapeDtypeStruct((nb, sq, FS_WIDTH), BF16),
        grid=(nb, sq // tq),
        in_specs=[
            pl.BlockSpec((1, tq, FS_WIDTH), lambda b, i: (b, i, 0)),
            pl.BlockSpec((1, sq, FS_WIDTH), lambda b, i: (b, 0, 0)),
            pl.BlockSpec((1, FS_WIDTH, sq), lambda b, i: (b, 0, 0)),
            pl.BlockSpec((SB_SUB, SB_SUB), lambda b, i: (0, 0)),
            pl.BlockSpec((1, tq, FS_WIDTH), lambda b, i: (b, i, 0)),
        ],
        out_specs=pl.BlockSpec((1, tq, FS_WIDTH), lambda b, i: (b, i, 0)),
        scratch_shapes=[
            pltpu.VMEM((rows, FS_WIDTH), BF16),
            pltpu.VMEM((1, rows), F32),
            pltpu.VMEM((N_FS_HEADS, HEAD_DIM, tq), F32),
        ],
        compiler_params=pltpu.CompilerParams(
            dimension_semantics=("parallel", "arbitrary"), vmem_limit_bytes=VMEM_LIMIT),
        name="sb_attention_t",
    )(q, k, vt, tri_t, g)


def _mla_t_kernel(q_ref, k_ref, vt_ref, wv_ref, g_ref, o_ref, m_sc, acc_sc, *, tq, tk):
    qi = pl.program_id(1)
    width = MLA_GROUP * tq
    n_groups = MLA_HEADS // MLA_GROUP
    m_sc[...] = jnp.full(m_sc.shape, NEG, F32)
    acc_sc[...] = jnp.zeros(acc_sc.shape, F32)

    def tile(j, masked):
        start = pl.multiple_of(j * tk, tk)
        k = k_ref[0, pl.ds(start, tk), :]
        vt = vt_ref[0, :, pl.ds(start, tk)]
        if masked:
            k_pos = j * tk + lax.broadcasted_iota(jnp.int32, (tk, 1), 0)
            q_pos = _query_positions(0, qi, tq, MLA_GROUP)
            mask = (k_pos >> CHUNK_SHIFT) <= (q_pos >> CHUNK_SHIFT)

        def scores(g):
            return _dot_nt(k, q_ref[0, g * MLA_GROUP:(g + 1) * MLA_GROUP].reshape(width, 256))

        s_next = scores(0)
        for g in range(n_groups):
            s_t = s_next
            if g + 1 < n_groups:
                s_next = scores(g + 1)
            cols = slice(g * width, (g + 1) * width)
            if masked:
                s_t = jnp.where(mask, s_t, NEG)
            m_prev = m_sc[:, cols]
            m_new = jnp.maximum(m_prev, jnp.max(s_t, axis=0, keepdims=True))
            p_t = jnp.exp2(s_t - m_new)
            alpha = jnp.exp2(m_prev - m_new)
            acc_sc[:, cols] = alpha * acc_sc[:, cols] + _dot(vt, p_t.astype(BF16))
            m_sc[:, cols] = m_new

    visible_end = ((((qi + 1) * tq - 1) >> CHUNK_SHIFT) + 1) << CHUNK_SHIFT
    j_last = (visible_end - 1) >> _log2(tk)

    def step(j, carry):
        tile(j, False)
        return carry

    lax.fori_loop(0, j_last, step, 0)
    tile(j_last, True)

    o_lat_t = acc_sc[0:KV_LORA, :] / acc_sc[KV_LORA:KV_LORA + 1, :]
    o = jnp.zeros((tq, MLA_WIDTH), F32)
    for h in range(MLA_HEADS):
        o = o + _dot(o_lat_t[:, h * tq:(h + 1) * tq].T.astype(BF16), wv_ref[h])
    o_ref[0] = (o * _silu(g_ref[0])).astype(BF16)


def _mla_attention_t(qcat, kcat, ckv_t, wv, g, *, tq, tk):
    nb, _, sq, _ = qcat.shape
    rows = MLA_HEADS * tq
    kern = functools.partial(_mla_t_kernel, tq=tq, tk=tk)
    return pl.pallas_call(
        kern,
        out_shape=jax.ShapeDtypeStruct((nb, sq, MLA_WIDTH), BF16),
        grid=(nb, sq // tq),
        in_specs=[
            pl.BlockSpec((1, MLA_HEADS, tq, 256), lambda b, i: (b, 0, i, 0)),
            pl.BlockSpec((1, sq, 256), lambda b, i: (b, 0, 0)),
            pl.BlockSpec((1, KV_LORA + ONES_ROWS, sq), lambda b, i: (b, 0, 0)),
            pl.BlockSpec((MLA_HEADS, KV_LORA, MLA_WIDTH), lambda b, i: (0, 0, 0)),
            pl.BlockSpec((1, tq, MLA_WIDTH), lambda b, i: (b, i, 0)),
        ],
        out_specs=pl.BlockSpec((1, tq, MLA_WIDTH), lambda b, i: (b, i, 0)),
        scratch_shapes=[
            pltpu.VMEM((1, rows), F32),
            pltpu.VMEM((KV_LORA + ONES_ROWS, rows), F32),
        ],
        compiler_params=pltpu.CompilerParams(
            dimension_semantics=("parallel", "arbitrary"), vmem_limit_bytes=VMEM_LIMIT),
        name="mla_attention_t",
    )(qcat, kcat, ckv_t, wv, g)


def _out_kernel(yf_ref, ys_ref, ym_ref, w_ref, x_ref, mod_ref, gpost_ref, o_ref):
    d = D_MODEL
    y = (_dot(yf_ref[0], w_ref[0:256, :]) + _dot(ys_ref[0], w_ref[256:512, :])
         + _dot(ym_ref[0], w_ref[512:1024, :]))
    gate = mod_ref[0][:, 2 * d:3 * d]
    o_ref[0] = x_ref[0] + gate * _rms(y, gpost_ref[...])


def _output(yf, ys, ym, w_out, x, mod_rows, g_post, tm):
    nb, s, d = x.shape
    mrows = mod_rows.shape[1]
    mblk = 1 if mrows == 1 else tm
    return pl.pallas_call(
        _out_kernel,
        out_shape=jax.ShapeDtypeStruct((nb, s, d), F32),
        grid=(nb, s // tm),
        in_specs=[
            pl.BlockSpec((1, tm, 256), lambda b, i: (b, i, 0)),
            pl.BlockSpec((1, tm, 256), lambda b, i: (b, i, 0)),
            pl.BlockSpec((1, tm, 512), lambda b, i: (b, i, 0)),
            pl.BlockSpec((d, d), lambda b, i: (0, 0)),
            pl.BlockSpec((1, tm, d), lambda b, i: (b, i, 0)),
            pl.BlockSpec((1, mblk, 3 * d),
                         (lambda b, i: (b, 0, 0)) if mrows == 1 else (lambda b, i: (b, i, 0))),
            pl.BlockSpec((1, d), lambda b, i: (0, 0)),
        ],
        out_specs=pl.BlockSpec((1, tm, d), lambda b, i: (b, i, 0)),
        compiler_params=pltpu.CompilerParams(
            dimension_semantics=("parallel", "arbitrary"), vmem_limit_bytes=VMEM_LIMIT),
        name="output_projection",
    )(yf, ys, ym, w_out, x, mod_rows, g_post)


def _layer_weights(l, g_pre, g_post, w_in, b_f, g_q_a, w_uq, g_kv_a, w_uk, w_uv, w_out):
    o = IN_OFFSETS
    w = w_in[l]

    def cols(i):
        return w[:, o[i]:o[i + 1]]

    kpe = cols(11)
    half = MLA_ROPE // 2
    zeros = lambda n: jnp.zeros((D_MODEL, n), F32)
    grp = jnp.concatenate([kpe, cols(3), zeros(64 - MLA_ROPE - 4),
                           -kpe[:, half:], kpe[:, :half], zeros(64 - MLA_ROPE)], axis=1)
    w_main = jnp.concatenate([cols(0), cols(1), cols(2), cols(4), cols(5), cols(6), cols(7), cols(8),
                              cols(9), cols(10), grp, cols(12)], axis=1).astype(BF16)
    bf_row = jnp.zeros((1, LANES), F32).at[0, MLA_ROPE:MLA_ROPE + 4].set(b_f[l])

    uq = w_uq[l].reshape(Q_LORA, MLA_HEADS, MLA_NOPE + MLA_ROPE)
    w_nope = uq[:, :, :MLA_NOPE].reshape(Q_LORA, MLA_HEADS * MLA_NOPE).astype(BF16)
    x1 = uq[:, :, MLA_NOPE:MLA_NOPE + half]
    x2 = uq[:, :, MLA_NOPE + half:]
    zpad = jnp.zeros((Q_LORA, MLA_HEADS, LANES - MLA_ROPE), F32)
    w_xy = jnp.concatenate([x1, x2, zpad, -x2, x1, zpad], axis=2)
    w_xy = jnp.transpose(w_xy, (1, 0, 2)).astype(BF16)

    ukt = jnp.transpose(w_uk[l], (1, 2, 0))
    zk = jnp.zeros_like(ukt)
    even = jnp.concatenate([ukt, zk], axis=1)
    odd = jnp.concatenate([zk, ukt], axis=1)
    is_odd = (jnp.arange(MLA_HEADS) % 2 == 1)[:, None, None]
    w_k = jnp.where(is_odd, odd, even).astype(BF16)

    uvt = jnp.transpose(w_uv[l], (1, 0, 2))
    sel = (jnp.arange(MLA_HEADS)[:, None] == jnp.arange(MLA_HEADS)[None, :]).astype(F32)
    w_v = (uvt[:, :, None, :] * sel[:, None, :, None]).reshape(MLA_HEADS, KV_LORA, MLA_WIDTH)
    return dict(g_pre=g_pre[l][None], g_post=g_post[l][None], w_main=w_main, bf_row=bf_row,
                g_q=g_q_a[l][None], w_nope=w_nope, w_xy=w_xy, w_k=w_k, g_kv=g_kv_a[l][None],
                w_v=w_v.astype(BF16), w_out=w_out[l].astype(BF16))


def _rope_tables(pos):
    half = MLA_ROPE // 2
    inv = ROPE_THETA ** (-jnp.arange(half, dtype=F32) / half)
    ang = pos.astype(F32)[:, None] * inv[None, :]
    pad = jnp.zeros((pos.shape[0], LANES - MLA_ROPE), F32)
    cos = jnp.concatenate([jnp.cos(ang), jnp.cos(ang), pad], axis=1)
    sin = jnp.concatenate([jnp.sin(ang), jnp.sin(ang), pad], axis=1)
    return cos, sin


def _pad_rows(a, total):
    pad = total - a.shape[1]
    if pad == 0:
        return a
    return jnp.concatenate([a, jnp.zeros((a.shape[0], pad) + a.shape[2:], a.dtype)], axis=1)


def _forget_cumsum(logf_all, skv):
    nb, n, _ = logf_all.shape
    length = -(-max(n, skv) // 1024) * 1024
    rows = jnp.transpose(_pad_rows(logf_all, length), (0, 2, 1)).reshape(nb * N_FS_HEADS, length)
    frow = _cumsum_rows(rows).reshape(nb, N_FS_HEADS, length)[:, :, :skv]
    return frow, jnp.transpose(frow, (0, 2, 1))


def _stream_layer(x, mod_rows, cos, sin, lw, past_rows, *, tm, past, fs_tiles, mla_tiles, batch):
    (fq_b, fk_f, fk_b, fv_f, fv_b, fg_f, sq_b, sk_f, sk_b, sv_f, sv_b, sg_f,
     qcat_b, ckv_f, kcat_b, mg_f, gout_f, fv_t, sv_t, ckv_t) = _projection(x, mod_rows, cos, sin, lw, tm)
    nb, rows, _ = x.shape
    seq = nb * rows // batch

    def per_batch(a):
        return a.reshape((batch, seq) + a.shape[2:])

    fq_b, fk_f, fk_b, fv_f, fv_b, fg_f, sq_b, sk_f, sk_b, sv_f, sv_b, sg_f, ckv_f, kcat_b, mg_f, gout_f = [
        per_batch(a) for a in (fq_b, fk_f, fk_b, fv_f, fv_b, fg_f, sq_b, sk_f, sk_b, sv_f, sv_b, sg_f,
                               ckv_f, kcat_b, mg_f, gout_f)]
    qcat_b = jnp.transpose(qcat_b.reshape(nb, MLA_HEADS, batch // nb, seq, 256),
                           (0, 2, 1, 3, 4)).reshape(batch, MLA_HEADS, seq, 256)
    kpe_f = gout_f[:, :, :MLA_ROPE]
    logf = gout_f[:, :, MLA_ROPE:MLA_ROPE + N_FS_HEADS]
    new = (fk_f, fv_f, logf, sk_f, sv_f, ckv_f, kpe_f)

    n_kv = past + seq
    tq_fs, tk_fs = fs_tiles
    tq_m, tk_m = mla_tiles
    skv = -(-n_kv // max(tk_fs, tk_m)) * max(tk_fs, tk_m)
    if past_rows is None:
        fk_a, fv_a, sk_a, sv_a, kc_a, logf_a = fk_b, fv_b, sk_b, sv_b, kcat_b, logf
    else:
        c_fk, c_fv, c_logf, c_sk, c_sv, c_ckv, c_kpe = past_rows

        def join(c, n):
            c = c.reshape(batch, past, -1).astype(BF16)
            return _pad_rows(jnp.concatenate([c, n], axis=1), skv)

        fk_a, fv_a, sk_a, sv_a = join(c_fk, fk_b), join(c_fv, fv_b), join(c_sk, sk_b), join(c_sv, sv_b)
        c_kc = jnp.concatenate([c_ckv, c_kpe, jnp.ones((batch, past, 1), F32),
                                jnp.zeros((batch, past, LANES - MLA_ROPE - 1), F32)], axis=2)
        kc_a = join(c_kc, kcat_b)
        logf_a = jnp.concatenate([c_logf, logf], axis=1)
    frow, fcol = _forget_cumsum(logf_a, skv)

    if past_rows is None:
        y_fox = _fox_attention_t(fq_b, fk_a, fv_t, fcol, frow, fg_f, tq=tq_fs, tk=tk_fs)
        y_sb = _sb_attention_t(sq_b, sk_a, sv_t, sg_f, tq=tq_fs, tk=tk_fs)
        y_mla = _mla_attention_t(qcat_b, kc_a, ckv_t, lw["w_v"], mg_f, tq=tq_m, tk=tk_m)
    else:
        y_fox = _fox_attention(fq_b, fk_a, fv_a, fcol, frow, fg_f, past=past, tq=tq_fs, tk=tk_fs)
        y_sb = _sb_attention(sq_b, sk_a, sv_a, sg_f, past=past, tq=tq_fs, tk=tk_fs)
        y_mla = _mla_attention(qcat_b, kc_a, lw["w_v"], mg_f, past=past, n_kv=n_kv, tq=tq_m, tk=tk_m)

    def per_block(a):
        return a.reshape((nb, rows) + a.shape[2:])

    x_new = _output(per_block(y_fox), per_block(y_sb), per_block(y_mla), lw["w_out"], x, mod_rows,
                    lw["g_post"], tm)
    return x_new, new


def kernel(x_prompt, x_sample, c_prompt, c_sample, cache_fox_k, cache_fox_v, cache_fox_logf, cache_sb_k, cache_sb_v, cache_mla_ckv, cache_mla_kpe, g_pre, g_post, w_ada, b_ada, w_in, b_f, g_q_a, w_uq, g_kv_a, w_uk, w_uv, w_out):
    batch, seq, d = x_prompt.shape
    dec_batch, dec_seq, _ = x_sample.shape
    past_len = cache_fox_k.shape[2]
    dec_rows = dec_batch * dec_seq

    mod = _modulation(jnp.concatenate([c_prompt, c_sample], axis=0), w_ada, b_ada)
    cos_p, sin_p = _rope_tables(jnp.arange(seq, dtype=jnp.int32))
    pos_s = past_len + (jnp.arange(dec_rows, dtype=jnp.int32) % dec_seq)
    cos_s, sin_s = _rope_tables(pos_s)
    skv_s = -(-(past_len + dec_seq) // SB_SUB) * SB_SUB

    y_p = x_prompt
    y_s = x_sample.reshape(1, dec_rows, d)
    rows_p, rows_s = [], []
    for l in range(DEPTH):
        lw = _layer_weights(l, g_pre, g_post, w_in, b_f, g_q_a, w_uq, g_kv_a, w_uk, w_uv, w_out)
        mod_p = mod[l, :batch][:, None, :]
        mod_s = jnp.repeat(mod[l, batch:], dec_seq, axis=0)[None]
        y_p, new_p = _stream_layer(y_p, mod_p, cos_p, sin_p, lw, None, tm=512, past=0,
                                   fs_tiles=(256, 512), mla_tiles=(256, 512), batch=batch)
        past_rows = (cache_fox_k[l], cache_fox_v[l], cache_fox_logf[l], cache_sb_k[l], cache_sb_v[l],
                     cache_mla_ckv[l], cache_mla_kpe[l])
        y_s, new_s = _stream_layer(y_s, mod_s, cos_s, sin_s, lw, past_rows, tm=dec_rows, past=past_len,
                                   fs_tiles=(dec_seq, skv_s), mla_tiles=(dec_seq, skv_s),
                                   batch=dec_batch)
        rows_p.append(new_p)
        rows_s.append(new_s)

    def stack(rows, idx, shape_tail):
        a = jnp.stack([r[idx] for r in rows])
        return a.reshape(a.shape[:3] + shape_tail)

    heads = (N_FS_HEADS, HEAD_DIM)
    tails = (heads, heads, (N_FS_HEADS,), heads, heads, (KV_LORA,), (MLA_ROPE,))
    outs_p = [stack(rows_p, i, t) for i, t in enumerate(tails)]
    outs_s = [stack(rows_s, i, t) for i, t in enumerate(tails)]
    return (y_p, y_s.reshape(dec_batch, dec_seq, d), *outs_p, *outs_s)
```

```python
import functools

import numpy as np
import jax
import jax.numpy as jnp
from jax import lax
from jax.experimental import pallas as pl
from jax.experimental.pallas import tpu as pltpu

D_MODEL = 1024
DEPTH = 2
CHUNK_SHIFT = 6
HEAD_DIM = 64
N_FS_HEADS = 4
FS_WIDTH = N_FS_HEADS * HEAD_DIM
MLA_HEADS = 8
MLA_GROUP = 2
MLA_NOPE = 64
MLA_ROPE = 32
MLA_V = 64
MLA_WIDTH = MLA_HEADS * MLA_V
Q_LORA = 256
KV_LORA = 128
ROPE_THETA = 10000.0
EPS = 1e-6
IN_SPLITS = (256, 256, 256, 4, 256, 256, 256, 256, 256, Q_LORA, KV_LORA, MLA_ROPE, MLA_WIDTH)
IN_OFFSETS = tuple(int(v) for v in np.cumsum((0,) + IN_SPLITS))

LANES = 128
SB_SUB = 256
CUMSUM_ROWS = 8
W_MAIN = 3072
NEG = -1e30
SB_DEAD = -110.0
FOX_DEAD = -110.0
NORM_SLACK = 1.0 + 2.0 ** -7
FS_SCALE = HEAD_DIM ** -0.5
MLA_SCALE = (MLA_NOPE + MLA_ROPE) ** -0.5
MLA_EXP2_SCALE = MLA_SCALE * float(np.log2(np.e))
ONES_LANE = KV_LORA + MLA_ROPE
ONES_ROWS = 16
VMEM_LIMIT = 56 * 1024 * 1024

BF16 = jnp.bfloat16
F32 = jnp.float32


def _dot(a, b):
    return jnp.dot(a, b, preferred_element_type=F32)


def _dot_nt(a, b):
    return lax.dot_general(a, b, (((1,), (1,)), ((), ())), preferred_element_type=F32)


def _silu(g):
    return g / (1.0 + jnp.exp(-g))


def _rms(x, g):
    return x * lax.rsqrt(jnp.mean(x * x, axis=-1, keepdims=True) + EPS) * g


def _mod_kernel(c_ref, w_ref, b_ref, o_ref):
    a = _silu(c_ref[...]).astype(BF16)
    o_ref[0] = _dot(a, w_ref[0].astype(BF16)) + b_ref[0]


def _modulation(c_all, w_ada, b_ada):
    n = c_all.shape[0]
    d = D_MODEL
    return pl.pallas_call(
        _mod_kernel,
        out_shape=jax.ShapeDtypeStruct((DEPTH, n, 3 * d), F32),
        grid=(DEPTH, 3),
        in_specs=[
            pl.BlockSpec((n, d), lambda l, j: (0, 0)),
            pl.BlockSpec((1, d, d), lambda l, j: (l, 0, j)),
            pl.BlockSpec((1, 1, d), lambda l, j: (l, 0, j)),
        ],
        out_specs=pl.BlockSpec((1, n, d), lambda l, j: (l, 0, j)),
        compiler_params=pltpu.CompilerParams(
            dimension_semantics=("arbitrary", "arbitrary"), vmem_limit_bytes=VMEM_LIMIT),
        name="modulation",
    )(c_all, w_ada, b_ada.reshape(DEPTH, 1, 3 * d))


def _proj_kernel(x_ref, mod_ref, gpre_ref, w_ref, bf_ref, gq_ref, wn_ref, wxy_ref, wk_ref,
                 gkv_ref, cos_ref, sin_ref,
                 fq_b, fk_f, fk_b, fv_f, fv_b, fg_f, sq_b, sk_f, sk_b, sv_f, sv_b, sg_f,
                 qcat_b, ckv_f, kcat_b, mg_f, gout_f, fv_t, sv_t, ckv_t):
    d = D_MODEL
    x = x_ref[0]
    mod = mod_ref[0]
    shift = mod[:, 0:d]
    scale = mod[:, d:2 * d]
    h = _rms(x, gpre_ref[...]) * (1.0 + scale) + shift
    hb = h.astype(BF16)

    def proj(a, b):
        return _dot(hb, w_ref[:, a:b])

    fq_b[0] = (proj(0, 256) * FS_SCALE).astype(BF16)
    t = proj(256, 512)
    fk_f[0] = t
    fk_b[0] = t.astype(BF16)
    t = proj(512, 768)
    fv_f[0] = t
    fv_b[0] = t.astype(BF16)
    fv_t[0] = t.T.astype(BF16)
    fg_f[0] = proj(768, 1024)
    sq_b[0] = (proj(1024, 1280) * FS_SCALE).astype(BF16)
    t = proj(1280, 1536)
    sk_f[0] = t
    sk_b[0] = t.astype(BF16)
    t = proj(1536, 1792)
    sv_f[0] = t
    sv_b[0] = t.astype(BF16)
    sv_t[0] = t.T.astype(BF16)
    sg_f[0] = proj(1792, 2048)
    mg_f[0] = proj(2560, 3072)

    cos = cos_ref[...]
    sin = sin_ref[...]

    cg = proj(2304, 2560)
    ckvn = _rms(cg[:, 0:KV_LORA], gkv_ref[...])
    ckv_f[0] = ckvn
    ckv_t[0] = jnp.concatenate([ckvn.T, jnp.ones((ONES_ROWS, ckvn.shape[0]), F32)], axis=0).astype(BF16)
    grp = cg[:, KV_LORA:2 * KV_LORA]
    rope_k = grp * cos + pltpu.roll(grp, 64, axis=1) * sin
    zf = grp + bf_ref[...]
    logf = jnp.minimum(zf, 0.0) - jnp.log(1.0 + jnp.exp(-jnp.abs(zf)))
    lane = lax.broadcasted_iota(jnp.int32, grp.shape, 1)
    gout_f[0] = jnp.where(lane < MLA_ROPE, rope_k, logf)
    ones_lane = jnp.where(lane == ONES_LANE - KV_LORA, 1.0, 0.0)
    kcat_b[0] = jnp.concatenate([ckvn, rope_k + ones_lane], axis=1).astype(BF16)

    cqn = _rms(proj(2048, 2304), gq_ref[...]).astype(BF16)
    qn = _dot(cqn, wn_ref[...]).astype(BF16)
    for hh in range(MLA_HEADS):
        pair = qn[:, (hh // 2) * LANES:(hh // 2 + 1) * LANES]
        qlat = _dot(pair, wk_ref[hh])
        xy = _dot(cqn, wxy_ref[hh])
        rope_q = xy[:, 0:LANES] * cos + xy[:, LANES:2 * LANES] * sin
        qcat_b[0, hh] = (jnp.concatenate([qlat, rope_q], axis=1) * MLA_EXP2_SCALE).astype(BF16)


def _projection(x, mod_rows, cos, sin, lw, tm):
    nb, s, d = x.shape
    mrows = mod_rows.shape[1]
    mblk = 1 if mrows == 1 else tm
    grid = (nb, s // tm)

    def row(width, dtype):
        return (jax.ShapeDtypeStruct((nb, s, width), dtype),
                pl.BlockSpec((1, tm, width), lambda b, i: (b, i, 0)))

    def col(width):
        return (jax.ShapeDtypeStruct((nb, width, s), BF16),
                pl.BlockSpec((1, width, tm), lambda b, i: (b, 0, i)))

    outs = [row(256, BF16), row(256, F32), row(256, BF16), row(256, F32), row(256, BF16),
            row(256, F32), row(256, BF16), row(256, F32), row(256, BF16), row(256, F32),
            row(256, BF16), row(256, F32),
            (jax.ShapeDtypeStruct((nb, MLA_HEADS, s, 256), BF16),
             pl.BlockSpec((1, MLA_HEADS, tm, 256), lambda b, i: (b, 0, i, 0))),
            row(KV_LORA, F32), row(256, BF16), row(MLA_WIDTH, F32), row(LANES, F32),
            col(256), col(256), col(KV_LORA + ONES_ROWS)]

    def full(a):
        nd = a.ndim
        return pl.BlockSpec(a.shape, lambda b, i: (0,) * nd)

    consts = [lw["g_pre"], lw["w_main"], lw["bf_row"], lw["g_q"], lw["w_nope"], lw["w_xy"],
              lw["w_k"], lw["g_kv"]]
    in_specs = ([pl.BlockSpec((1, tm, d), lambda b, i: (b, i, 0)),
                 pl.BlockSpec((1, mblk, 3 * d),
                              (lambda b, i: (b, 0, 0)) if mrows == 1 else (lambda b, i: (b, i, 0)))]
                + [full(a) for a in consts]
                + [pl.BlockSpec((tm, LANES), lambda b, i: (i, 0)),
                   pl.BlockSpec((tm, LANES), lambda b, i: (i, 0))])
    return pl.pallas_call(
        _proj_kernel,
        out_shape=[o[0] for o in outs],
        grid=grid,
        in_specs=in_specs,
        out_specs=[o[1] for o in outs],
        compiler_params=pltpu.CompilerParams(
            dimension_semantics=("parallel", "arbitrary"), vmem_limit_bytes=VMEM_LIMIT),
        name="projection",
    )(x, mod_rows, *consts, cos, sin)


def _cumsum_kernel(x_ref, u_ref, l_ref, o_ref):
    g, n, _ = x_ref.shape
    w_all = jnp.dot(x_ref[...].reshape(g * n, LANES), u_ref[...], preferred_element_type=F32,
                    precision=lax.Precision.HIGHEST)
    for i in range(g):
        w = w_all[i * n:(i + 1) * n]
        tot = jnp.broadcast_to(w[:, LANES - 1:LANES], w.shape)
        off = jnp.dot(l_ref[...], tot, preferred_element_type=F32, precision=lax.Precision.HIGHEST)
        o_ref[i] = w + off


def _cumsum_rows(x):
    r, length = x.shape
    n = length // LANES
    u = jnp.asarray(np.triu(np.ones((LANES, LANES), np.float32)))
    lo = jnp.asarray(np.tril(np.ones((n, n), np.float32), -1))
    out = pl.pallas_call(
        _cumsum_kernel,
        out_shape=jax.ShapeDtypeStruct((r, n, LANES), F32),
        grid=(r // CUMSUM_ROWS,),
        in_specs=[pl.BlockSpec((CUMSUM_ROWS, n, LANES), lambda i: (i, 0, 0)),
                  pl.BlockSpec((LANES, LANES), lambda i: (0, 0)),
                  pl.BlockSpec((n, n), lambda i: (0, 0))],
        out_specs=pl.BlockSpec((CUMSUM_ROWS, n, LANES), lambda i: (i, 0, 0)),
        compiler_params=pltpu.CompilerParams(dimension_semantics=("arbitrary",)),
        name="cumsum_logf",
    )(x.reshape(r, n, LANES), u, lo)
    return out.reshape(r, length)


def _schedule(nq, kmax_fn, reverse):
    qi_l, kj_l, first_l, last_l = [], [], [], []
    for qi in range(nq):
        ks = list(range(kmax_fn(qi) + 1))
        if reverse:
            ks = ks[::-1]
        for n, kj in enumerate(ks):
            qi_l.append(qi)
            kj_l.append(kj)
            first_l.append(int(n == 0))
            last_l.append(int(n == len(ks) - 1))
    return tuple(jnp.asarray(np.asarray(a, np.int32)) for a in (qi_l, kj_l, first_l, last_l))


def _head_lane_mask(shape, h):
    lane = lax.broadcasted_iota(jnp.int32, shape, 1)
    return (lane >= HEAD_DIM * h) & (lane < HEAD_DIM * (h + 1))


def _fox_kernel(qt, kt, ft, lt, q_ref, k_ref, v_ref, fc_ref, fr_ref, g_ref, o_ref,
                qm_sc, m_sc, l_sc, acc_sc, *, past, tq, tk):
    step = pl.program_id(1)
    qi = qt[step]
    kj = kt[step]

    @pl.when(ft[step] == 1)
    def _():
        q = q_ref[0]
        for h in range(N_FS_HEADS):
            qm_sc[h] = jnp.where(_head_lane_mask(q.shape, h), q, jnp.zeros_like(q))
        m_sc[...] = jnp.full(m_sc.shape, NEG, F32)
        l_sc[...] = jnp.zeros(l_sc.shape, F32)
        acc_sc[...] = jnp.zeros(acc_sc.shape, F32)

    k = k_ref[0]
    v = v_ref[0]
    q_pos = past + qi * tq + lax.broadcasted_iota(jnp.int32, (tq, 1), 0)
    k_pos = kj * tk + lax.broadcasted_iota(jnp.int32, (1, tk), 1)
    mask = k_pos <= q_pos
    fc = fc_ref[0]
    fr = fr_ref[0]
    for h in range(N_FS_HEADS):
        s = _dot_nt(qm_sc[h], k)
        s = s + (fc[:, h:h + 1] - fr[h:h + 1, :])
        s = jnp.where(mask, s, NEG)
        m_prev = m_sc[h]
        m_new = jnp.maximum(m_prev, jnp.max(s, axis=1, keepdims=True))
        p = jnp.exp(s - m_new)
        alpha = jnp.exp(m_prev - m_new)
        l_sc[h] = alpha * l_sc[h] + jnp.sum(p, axis=1, keepdims=True)
        acc_sc[h] = alpha * acc_sc[h] + _dot(p.astype(BF16), v)
        m_sc[h] = m_new

    @pl.when(lt[step] == 1)
    def _():
        o = jnp.zeros((tq, FS_WIDTH), F32)
        for h in range(N_FS_HEADS):
            o = jnp.where(_head_lane_mask(o.shape, h), acc_sc[h] / l_sc[h], o)
        o_ref[0] = (o * _silu(g_ref[0])).astype(BF16)


def _fox_attention(q, k, v, fcol, frow, g, *, past, tq, tk):
    nb, sq, _ = q.shape
    nq = sq // tq
    tabs = _schedule(nq, lambda qi: (past + (qi + 1) * tq - 1) // tk, False)
    nsteps = int(tabs[0].shape[0])
    qoff = past // tq
    kern = functools.partial(_fox_kernel, past=past, tq=tq, tk=tk)
    return pl.pallas_call(
        kern,
        out_shape=jax.ShapeDtypeStruct((nb, sq, FS_WIDTH), BF16),
        grid_spec=pltpu.PrefetchScalarGridSpec(
            num_scalar_prefetch=4,
            grid=(nb, nsteps),
            in_specs=[
                pl.BlockSpec((1, tq, FS_WIDTH), lambda b, s, qt, kt, ft, lt: (b, qt[s], 0)),
                pl.BlockSpec((1, tk, FS_WIDTH), lambda b, s, qt, kt, ft, lt: (b, kt[s], 0)),
                pl.BlockSpec((1, tk, FS_WIDTH), lambda b, s, qt, kt, ft, lt: (b, kt[s], 0)),
                pl.BlockSpec((1, tq, N_FS_HEADS), lambda b, s, qt, kt, ft, lt: (b, qoff + qt[s], 0)),
                pl.BlockSpec((1, N_FS_HEADS, tk), lambda b, s, qt, kt, ft, lt: (b, 0, kt[s])),
                pl.BlockSpec((1, tq, FS_WIDTH), lambda b, s, qt, kt, ft, lt: (b, qt[s], 0)),
            ],
            out_specs=pl.BlockSpec((1, tq, FS_WIDTH), lambda b, s, qt, kt, ft, lt: (b, qt[s], 0)),
            scratch_shapes=[
                pltpu.VMEM((N_FS_HEADS, tq, FS_WIDTH), BF16),
                pltpu.VMEM((N_FS_HEADS, tq, 1), F32),
                pltpu.VMEM((N_FS_HEADS, tq, 1), F32),
                pltpu.VMEM((N_FS_HEADS, tq, FS_WIDTH), F32),
            ],
        ),
        compiler_params=pltpu.CompilerParams(
            dimension_semantics=("parallel", "arbitrary"), vmem_limit_bytes=VMEM_LIMIT),
        name="fox_attention",
    )(*tabs, q, k, v, fcol, frow, g)


def _sb_kernel(qt, kt, ft, lt, q_ref, k_ref, v_ref, tri_ref, g_ref, o_ref,
               qm_sc, r_sc, acc_sc, *, past, tq, tk):
    step = pl.program_id(1)
    qi = qt[step]
    kj = kt[step]

    @pl.when(ft[step] == 1)
    def _():
        q = q_ref[0]
        for h in range(N_FS_HEADS):
            qm_sc[h] = jnp.where(_head_lane_mask(q.shape, h), q, jnp.zeros_like(q))
        r_sc[...] = jnp.zeros(r_sc.shape, F32)
        acc_sc[...] = jnp.zeros(acc_sc.shape, F32)

    q_pos = past + qi * tq + lax.broadcasted_iota(jnp.int32, (tq, 1), 0)

    def sub_block(sub):
        tri = tri_ref[...]
        k = k_ref[0, sub * SB_SUB:(sub + 1) * SB_SUB, :]
        v = v_ref[0, sub * SB_SUB:(sub + 1) * SB_SUB, :]
        k_pos = kj * tk + sub * SB_SUB + lax.broadcasted_iota(jnp.int32, (1, SB_SUB), 1)
        mask = k_pos < q_pos
        for h in range(N_FS_HEADS):
            z = _dot_nt(qm_sc[h], k)
            sp = jnp.maximum(z, 0.0) + jnp.log(1.0 + jnp.exp(-jnp.abs(z)))
            l1m = jnp.where(mask, -sp, 0.0)
            hi = l1m.astype(BF16)
            lo = (l1m - hi.astype(F32)).astype(BF16)
            c = _dot(hi, tri) + _dot(lo, tri)
            r = r_sc[h]
            a = jnp.where(mask, jnp.exp(z + (c + r)), 0.0)
            acc_sc[h] = acc_sc[h] + _dot(a.astype(BF16), v)
            r_sc[h] = r + c[:, 0:1]

    for sub in reversed(range(tk // SB_SUB)):
        pl.when(jnp.max(r_sc[...]) > SB_DEAD)(functools.partial(sub_block, sub))

    @pl.when(lt[step] == 1)
    def _():
        o = jnp.zeros((tq, FS_WIDTH), F32)
        for h in range(N_FS_HEADS):
            o = jnp.where(_head_lane_mask(o.shape, h), acc_sc[h], o)
        o_ref[0] = (o * _silu(g_ref[0])).astype(BF16)


def _sb_attention(q, k, v, g, *, past, tq, tk):
    nb, sq, _ = q.shape
    nq = sq // tq
    tabs = _schedule(nq, lambda qi: (past + (qi + 1) * tq - 1) // tk, True)
    nsteps = int(tabs[0].shape[0])
    tri = jnp.asarray(np.tril(np.ones((SB_SUB, SB_SUB), np.float32))).astype(BF16)
    kern = functools.partial(_sb_kernel, past=past, tq=tq, tk=tk)
    return pl.pallas_call(
        kern,
        out_shape=jax.ShapeDtypeStruct((nb, sq, FS_WIDTH), BF16),
        grid_spec=pltpu.PrefetchScalarGridSpec(
            num_scalar_prefetch=4,
            grid=(nb, nsteps),
            in_specs=[
                pl.BlockSpec((1, tq, FS_WIDTH), lambda b, s, qt, kt, ft, lt: (b, qt[s], 0)),
                pl.BlockSpec((1, tk, FS_WIDTH), lambda b, s, qt, kt, ft, lt: (b, kt[s], 0)),
                pl.BlockSpec((1, tk, FS_WIDTH), lambda b, s, qt, kt, ft, lt: (b, kt[s], 0)),
                pl.BlockSpec((SB_SUB, SB_SUB), lambda b, s, qt, kt, ft, lt: (0, 0)),
                pl.BlockSpec((1, tq, FS_WIDTH), lambda b, s, qt, kt, ft, lt: (b, qt[s], 0)),
            ],
            out_specs=pl.BlockSpec((1, tq, FS_WIDTH), lambda b, s, qt, kt, ft, lt: (b, qt[s], 0)),
            scratch_shapes=[
                pltpu.VMEM((N_FS_HEADS, tq, FS_WIDTH), BF16),
                pltpu.VMEM((N_FS_HEADS, tq, 1), F32),
                pltpu.VMEM((N_FS_HEADS, tq, FS_WIDTH), F32),
            ],
        ),
        compiler_params=pltpu.CompilerParams(
            dimension_semantics=("parallel", "arbitrary"), vmem_limit_bytes=VMEM_LIMIT),
        name="sb_attention",
    )(*tabs, q, k, v, tri, g)


def _mla_kernel(qt, kt, ft, lt, q_ref, k_ref, wv_ref, g_ref, o_ref,
                m_sc, acc_sc, *, past, n_kv, tq, tk):
    step = pl.program_id(1)
    qi = qt[step]
    kj = kt[step]
    rows = MLA_HEADS * tq

    @pl.when(ft[step] == 1)
    def _():
        m_sc[...] = jnp.full(m_sc.shape, NEG, F32)
        acc_sc[...] = jnp.zeros(acc_sc.shape, F32)

    k = k_ref[0]
    q_pos = past + qi * tq + lax.broadcasted_iota(jnp.int32, (tq, 1), 0)
    k_pos = kj * tk + lax.broadcasted_iota(jnp.int32, (1, tk), 1)
    mask = ((k_pos >> CHUNK_SHIFT) <= (q_pos >> CHUNK_SHIFT)) & (k_pos < n_kv)
    s = _dot_nt(q_ref[0].reshape(rows, 256), k).reshape(MLA_HEADS, tq, tk)
    s = jnp.where(mask[None], s, NEG)
    m_prev = m_sc[...]
    m_new = jnp.maximum(m_prev, jnp.max(s, axis=2, keepdims=True))
    p = jnp.exp2(s - m_new)
    alpha = jnp.exp2(m_prev - m_new)
    pv = _dot(p.reshape(rows, tk).astype(BF16), k)
    acc_sc[...] = alpha.reshape(rows, 1) * acc_sc[...] + pv
    m_sc[...] = m_new

    @pl.when(lt[step] == 1)
    def _():
        acc = acc_sc[...]
        o_lat = (acc[:, 0:KV_LORA] / acc[:, ONES_LANE:ONES_LANE + 1]).astype(BF16)
        o = jnp.zeros((tq, MLA_WIDTH), F32)
        for h in range(MLA_HEADS):
            o = o + _dot(o_lat[h * tq:(h + 1) * tq], wv_ref[h])
        o_ref[0] = (o * _silu(g_ref[0])).astype(BF16)


def _mla_attention(qcat, kcat, wv, g, *, past, n_kv, tq, tk):
    nb, _, sq, _ = qcat.shape
    nq = sq // tq

    def kmax(qi):
        last_q = past + (qi + 1) * tq - 1
        end = min(((last_q >> CHUNK_SHIFT) + 1) << CHUNK_SHIFT, n_kv)
        return (end - 1) // tk

    tabs = _schedule(nq, kmax, False)
    nsteps = int(tabs[0].shape[0])
    kern = functools.partial(_mla_kernel, past=past, n_kv=n_kv, tq=tq, tk=tk)
    return pl.pallas_call(
        kern,
        out_shape=jax.ShapeDtypeStruct((nb, sq, MLA_WIDTH), BF16),
        grid_spec=pltpu.PrefetchScalarGridSpec(
            num_scalar_prefetch=4,
            grid=(nb, nsteps),
            in_specs=[
                pl.BlockSpec((1, MLA_HEADS, tq, 256), lambda b, s, qt, kt, ft, lt: (b, 0, qt[s], 0)),
                pl.BlockSpec((1, tk, 256), lambda b, s, qt, kt, ft, lt: (b, kt[s], 0)),
                pl.BlockSpec((MLA_HEADS, KV_LORA, MLA_WIDTH), lambda b, s, qt, kt, ft, lt: (0, 0, 0)),
                pl.BlockSpec((1, tq, MLA_WIDTH), lambda b, s, qt, kt, ft, lt: (b, qt[s], 0)),
            ],
            out_specs=pl.BlockSpec((1, tq, MLA_WIDTH), lambda b, s, qt, kt, ft, lt: (b, qt[s], 0)),
            scratch_shapes=[
                pltpu.VMEM((MLA_HEADS, tq, 1), F32),
                pltpu.VMEM((MLA_HEADS * tq, 256), F32),
            ],
        ),
        compiler_params=pltpu.CompilerParams(
            dimension_semantics=("parallel", "arbitrary"), vmem_limit_bytes=VMEM_LIMIT),
        name="mla_attention",
    )(*tabs, qcat, kcat, wv, g)


def _stack_heads(q, qm_sc, tq):
    for h in range(N_FS_HEADS):
        qm_sc[h * tq:(h + 1) * tq, :] = jnp.where(_head_lane_mask(q.shape, h), q, jnp.zeros_like(q))


def _query_positions(past, qi, tq, heads):
    lane = lax.broadcasted_iota(jnp.int32, (1, heads * tq), 1)
    return past + qi * tq + (lane & (tq - 1))


def _log2(n):
    assert n & (n - 1) == 0
    return n.bit_length() - 1


def _live_flag(j, value, floor):
    return jnp.logical_and(j >= 0, value > floor).astype(jnp.int32)


def _fox_t_kernel(q_ref, k_ref, vt_ref, fc_ref, fr_ref, g_ref, e_ref, o_ref,
                  qm_sc, fkb_sc, pm_sc, fq_sc, qn_sc, m_sc, l_sc, acc_sc, *, tq, tk):
    qi = pl.program_id(1)
    seq = k_ref.shape[1]
    groups = tq // LANES

    def per_head_row(x):
        return jnp.concatenate([x[:, (g // groups) * LANES:(g // groups + 1) * LANES]
                                for g in range(N_FS_HEADS * groups)], axis=1)

    @pl.when(qi == 0)
    def _():
        fc = fc_ref[0]
        for h in range(N_FS_HEADS):
            fkb_sc[h] = jnp.broadcast_to(fc[:, h:h + 1], (seq, LANES))
        running = jnp.zeros((1, N_FS_HEADS * LANES), F32)
        for j in range(seq // tk):
            kf = k_ref[0, j * tk:(j + 1) * tk, :].astype(F32)
            n2 = _dot((kf * kf).astype(BF16), e_ref[...])
            running = jnp.maximum(running, jnp.max(n2, axis=0, keepdims=True))
            pm_sc[j:j + 1, :] = running

    _stack_heads(q_ref[0], qm_sc, tq)
    fr = fr_ref[0]
    fq_sc[...] = jnp.concatenate([fr[h:h + 1, :] for h in range(N_FS_HEADS)], axis=1)
    qf = qm_sc[...].astype(F32)
    qn_sc[...] = _dot_nt(jnp.ones((8, FS_WIDTH), BF16), (qf * qf).astype(BF16))[0:1, :]
    m_sc[...] = jnp.full(m_sc.shape, NEG, F32)
    l_sc[...] = jnp.zeros(l_sc.shape, F32)
    acc_sc[...] = jnp.zeros(acc_sc.shape, F32)

    def tile(j, masked):
        start = pl.multiple_of(j * tk, tk)
        s_t = _dot_nt(k_ref[0, pl.ds(start, tk), :], qm_sc[...])
        fk = fkb_sc[:, pl.ds(start, tk), :]
        s_t = jnp.concatenate(
            [s_t[:, g * LANES:(g + 1) * LANES] - fk[g // groups] for g in range(N_FS_HEADS * groups)],
            axis=1) + fq_sc[...]
        if masked:
            k_pos = j * tk + lax.broadcasted_iota(jnp.int32, (tk, 1), 0)
            s_t = jnp.where(k_pos <= _query_positions(0, qi, tq, N_FS_HEADS), s_t, NEG)
        m_prev = m_sc[...]
        m_new = jnp.maximum(m_prev, jnp.max(s_t, axis=0, keepdims=True))
        p_t = jnp.exp(s_t - m_new)
        alpha = jnp.exp(m_prev - m_new)
        l_sc[...] = alpha * l_sc[...] + jnp.sum(p_t, axis=0, keepdims=True)
        m_sc[...] = m_new
        p_b = p_t.astype(BF16)
        for h in range(N_FS_HEADS):
            pv = _dot(vt_ref[0, h * HEAD_DIM:(h + 1) * HEAD_DIM, pl.ds(start, tk)],
                      p_b[:, h * tq:(h + 1) * tq])
            acc_sc[h] = alpha[:, h * tq:(h + 1) * tq] * acc_sc[h] + pv

    def live(j):
        jc = jnp.maximum(j, 0)
        pm = per_head_row(pm_sc[pl.ds(jc, 1), :])
        fend = per_head_row(jnp.concatenate(
            [fkb_sc[h, pl.ds((jc + 1) * tk - 1, 1), :] for h in range(N_FS_HEADS)], axis=1))
        bound = jnp.sqrt(qn_sc[...] * pm) * NORM_SLACK + fq_sc[...] - fend - m_sc[...]
        return _live_flag(j, jnp.max(bound), FOX_DEAD)

    j_diag = ((qi + 1) * tq - 1) >> _log2(tk)
    tile(j_diag, True)

    def step(carry):
        j = carry[0]
        tile(j, False)
        return j - 1, live(j - 1)

    lax.while_loop(lambda c: c[1] > 0, step, (j_diag - 1, live(j_diag - 1)))

    l = l_sc[...]
    o_t = jnp.concatenate([acc_sc[h] / l[:, h * tq:(h + 1) * tq] for h in range(N_FS_HEADS)], axis=0)
    o_ref[0] = (o_t.T * _silu(g_ref[0])).astype(BF16)


def _fox_attention_t(q, k, vt, fcol, frow, g, *, tq, tk):
    nb, sq, _ = q.shape
    rows = N_FS_HEADS * tq
    head_of_col = np.arange(FS_WIDTH)[:, None] // HEAD_DIM
    head_of_lane = np.arange(N_FS_HEADS * LANES)[None, :] // LANES
    expand = jnp.asarray((head_of_col == head_of_lane).astype(np.float32)).astype(BF16)
    kern = functools.partial(_fox_t_kernel, tq=tq, tk=tk)
    return pl.pallas_call(
        kern,
        out_shape=jax.ShapeDtypeStruct((nb, sq, FS_WIDTH), BF16),
        grid=(nb, sq // tq),
        in_specs=[
            pl.BlockSpec((1, tq, FS_WIDTH), lambda b, i: (b, i, 0)),
            pl.BlockSpec((1, sq, FS_WIDTH), lambda b, i: (b, 0, 0)),
            pl.BlockSpec((1, FS_WIDTH, sq), lambda b, i: (b, 0, 0)),
            pl.BlockSpec((1, sq, N_FS_HEADS), lambda b, i: (b, 0, 0)),
            pl.BlockSpec((1, N_FS_HEADS, tq), lambda b, i: (b, 0, i)),
            pl.BlockSpec((1, tq, FS_WIDTH), lambda b, i: (b, i, 0)),
            pl.BlockSpec(expand.shape, lambda b, i: (0, 0)),
        ],
        out_specs=pl.BlockSpec((1, tq, FS_WIDTH), lambda b, i: (b, i, 0)),
        scratch_shapes=[
            pltpu.VMEM((rows, FS_WIDTH), BF16),
            pltpu.VMEM((N_FS_HEADS, sq, LANES), F32),
            pltpu.VMEM((sq // tk, N_FS_HEADS * LANES), F32),
            pltpu.VMEM((1, rows), F32),
            pltpu.VMEM((1, rows), F32),
            pltpu.VMEM((1, rows), F32),
            pltpu.VMEM((1, rows), F32),
            pltpu.VMEM((N_FS_HEADS, HEAD_DIM, tq), F32),
        ],
        compiler_params=pltpu.CompilerParams(
            dimension_semantics=("parallel", "arbitrary"), vmem_limit_bytes=VMEM_LIMIT),
        name="fox_attention_t",
    )(q, k, vt, fcol, frow, g, expand)


def _sb_t_kernel(q_ref, k_ref, vt_ref, tri_ref, g_ref, o_ref, qm_sc, r_sc, acc_sc, *, tq, tk):
    qi = pl.program_id(1)
    _stack_heads(q_ref[0], qm_sc, tq)
    r_sc[...] = jnp.zeros(r_sc.shape, F32)
    acc_sc[...] = jnp.zeros(acc_sc.shape, F32)

    def tile(j, masked):
        start = pl.multiple_of(j * tk, tk)
        z_all = _dot_nt(k_ref[0, pl.ds(start, tk), :], qm_sc[...])
        tri_t = tri_ref[...]
        for sub in reversed(range(tk // SB_SUB)):
            z = z_all[sub * SB_SUB:(sub + 1) * SB_SUB, :]
            l1m = -(jnp.maximum(z, 0.0) + jnp.log(1.0 + jnp.exp(-jnp.abs(z))))
            if masked:
                k_pos = j * tk + sub * SB_SUB + lax.broadcasted_iota(jnp.int32, (SB_SUB, 1), 0)
                mask = k_pos < _query_positions(0, qi, tq, N_FS_HEADS)
                l1m = jnp.where(mask, l1m, 0.0)
            hi = l1m.astype(BF16)
            lo = (l1m - hi.astype(F32)).astype(BF16)
            c = _dot(tri_t, hi) + _dot(tri_t, lo)
            r = r_sc[...]
            a = jnp.exp(z + (c + r))
            if masked:
                a = jnp.where(mask, a, 0.0)
            a_b = a.astype(BF16)
            for h in range(N_FS_HEADS):
                acc_sc[h] = acc_sc[h] + _dot(
                    vt_ref[0, h * HEAD_DIM:(h + 1) * HEAD_DIM,
                           pl.ds(pl.multiple_of(start + sub * SB_SUB, SB_SUB), SB_SUB)],
                    a_b[:, h * tq:(h + 1) * tq])
            r_sc[...] = r + c[0:1, :]

    def live(j):
        return _live_flag(j, jnp.max(r_sc[...]), SB_DEAD)

    j_diag = ((qi + 1) * tq - 1) >> _log2(tk)
    tile(j_diag, True)

    def step(carry):
        j = carry[0]
        tile(j, False)
        return j - 1, live(j - 1)

    lax.while_loop(lambda c: c[1] > 0, step, (j_diag - 1, live(j_diag - 1)))

    o_t = jnp.concatenate([acc_sc[h] for h in range(N_FS_HEADS)], axis=0)
    o_ref[0] = (o_t.T * _silu(g_ref[0])).astype(BF16)


def _sb_attention_t(q, k, vt, g, *, tq, tk):
    nb, sq, _ = q.shape
    rows = N_FS_HEADS * tq
    tri_t = jnp.asarray(np.triu(np.ones((SB_SUB, SB_SUB), np.float32))).astype(BF16)
    kern = functools.partial(_sb_t_kernel, tq=tq, tk=tk)
    return pl.pallas_call(
        kern,
        out_shape=jax.ShapeDtypeStruct((nb, sq, FS_WIDTH), BF16),
        grid=(nb, sq // tq),
        in_specs=[
            pl.BlockSpec((1, tq, FS_WIDTH), lambda b, i: (b, i, 0)),
            pl.BlockSpec((1, sq, FS_WIDTH), lambda b, i: (b, 0, 0)),
            pl.BlockSpec((1, FS_WIDTH, sq), lambda b, i: (b, 0, 0)),
            pl.BlockSpec((SB_SUB, SB_SUB), lambda b, i: (0, 0)),
            pl.BlockSpec((1, tq, FS_WIDTH), lambda b, i: (b, i, 0)),
        ],
        out_specs=pl.BlockSpec((1, tq, FS_WIDTH), lambda b, i: (b, i, 0)),
        scratch_shapes=[
            pltpu.VMEM((rows, FS_WIDTH), BF16),
            pltpu.VMEM((1, rows), F32),
            pltpu.VMEM((N_FS_HEADS, HEAD_DIM, tq), F32),
        ],
        compiler_params=pltpu.CompilerParams(
            dimension_semantics=("parallel", "arbitrary"), vmem_limit_bytes=VMEM_LIMIT),
        name="sb_attention_t",
    )(q, k, vt, tri_t, g)


def _mla_t_kernel(q_ref, k_ref, vt_ref, wv_ref, g_ref, o_ref, m_sc, acc_sc, *, tq, tk):
    qi = pl.program_id(1)
    width = MLA_GROUP * tq
    n_groups = MLA_HEADS // MLA_GROUP
    m_sc[...] = jnp.full(m_sc.shape, NEG, F32)
    acc_sc[...] = jnp.zeros(acc_sc.shape, F32)

    def tile(j, masked, keys=tk):
        start = pl.multiple_of(j * tk, tk)
        k = k_ref[0, pl.ds(start, keys), :]
        vt = vt_ref[0, :, pl.ds(start, keys)]
        if masked:
            k_pos = j * tk + lax.broadcasted_iota(jnp.int32, (keys, 1), 0)
            q_pos = _query_positions(0, qi, tq, MLA_GROUP)
            mask = (k_pos >> CHUNK_SHIFT) <= (q_pos >> CHUNK_SHIFT)

        def scores(g):
            return _dot_nt(k, q_ref[0, g * MLA_GROUP:(g + 1) * MLA_GROUP].reshape(width, 256))

        s_next = scores(0)
        for g in range(n_groups):
            s_t = s_next
            if g + 1 < n_groups:
                s_next = scores(g + 1)
            cols = slice(g * width, (g + 1) * width)
            if masked:
                s_t = jnp.where(mask, s_t, NEG)
            m_prev = m_sc[:, cols]
            m_new = jnp.maximum(m_prev, jnp.max(s_t, axis=0, keepdims=True))
            p_t = jnp.exp2(s_t - m_new)
            alpha = jnp.exp2(m_prev - m_new)
            acc_sc[:, cols] = alpha * acc_sc[:, cols] + _dot(vt, p_t.astype(BF16))
            m_sc[:, cols] = m_new

    visible_end = ((((qi + 1) * tq - 1) >> CHUNK_SHIFT) + 1) << CHUNK_SHIFT
    j_last = (visible_end - 1) >> _log2(tk)

    def step(j, carry):
        tile(j, False)
        return carry

    lax.fori_loop(0, j_last, step, 0)
    short = visible_end - j_last * tk <= tk // 2
    pl.when(short)(lambda: tile(j_last, True, tk // 2))
    pl.when(jnp.logical_not(short))(lambda: tile(j_last, True))

    o_lat_t = acc_sc[0:KV_LORA, :] / acc_sc[KV_LORA:KV_LORA + 1, :]
    o = jnp.zeros((tq, MLA_WIDTH), F32)
    for h in range(MLA_HEADS):
        o = o + _dot(o_lat_t[:, h * tq:(h + 1) * tq].T.astype(BF16), wv_ref[h])
    o_ref[0] = (o * _silu(g_ref[0])).astype(BF16)


def _mla_attention_t(qcat, kcat, ckv_t, wv, g, *, tq, tk):
    nb, _, sq, _ = qcat.shape
    rows = MLA_HEADS * tq
    kern = functools.partial(_mla_t_kernel, tq=tq, tk=tk)
    return pl.pallas_call(
        kern,
        out_shape=jax.ShapeDtypeStruct((nb, sq, MLA_WIDTH), BF16),
        grid=(nb, sq // tq),
        in_specs=[
            pl.BlockSpec((1, MLA_HEADS, tq, 256), lambda b, i: (b, 0, i, 0)),
            pl.BlockSpec((1, sq, 256), lambda b, i: (b, 0, 0)),
            pl.BlockSpec((1, KV_LORA + ONES_ROWS, sq), lambda b, i: (b, 0, 0)),
            pl.BlockSpec((MLA_HEADS, KV_LORA, MLA_WIDTH), lambda b, i: (0, 0, 0)),
            pl.BlockSpec((1, tq, MLA_WIDTH), lambda b, i: (b, i, 0)),
        ],
        out_specs=pl.BlockSpec((1, tq, MLA_WIDTH), lambda b, i: (b, i, 0)),
        scratch_shapes=[
            pltpu.VMEM((1, rows), F32),
            pltpu.VMEM((KV_LORA + ONES_ROWS, rows), F32),
        ],
        compiler_params=pltpu.CompilerParams(
            dimension_semantics=("parallel", "arbitrary"), vmem_limit_bytes=VMEM_LIMIT),
        name="mla_attention_t",
    )(qcat, kcat, ckv_t, wv, g)


def _out_kernel(yf_ref, ys_ref, ym_ref, w_ref, x_ref, mod_ref, gpost_ref, o_ref):
    d = D_MODEL
    y = (_dot(yf_ref[0], w_ref[0:256, :]) + _dot(ys_ref[0], w_ref[256:512, :])
         + _dot(ym_ref[0], w_ref[512:1024, :]))
    gate = mod_ref[0][:, 2 * d:3 * d]
    o_ref[0] = x_ref[0] + gate * _rms(y, gpost_ref[...])


def _output(yf, ys, ym, w_out, x, mod_rows, g_post, tm):
    nb, s, d = x.shape
    mrows = mod_rows.shape[1]
    mblk = 1 if mrows == 1 else tm
    return pl.pallas_call(
        _out_kernel,
        out_shape=jax.ShapeDtypeStruct((nb, s, d), F32),
        grid=(nb, s // tm),
        in_specs=[
            pl.BlockSpec((1, tm, 256), lambda b, i: (b, i, 0)),
            pl.BlockSpec((1, tm, 256), lambda b, i: (b, i, 0)),
            pl.BlockSpec((1, tm, 512), lambda b, i: (b, i, 0)),
            pl.BlockSpec((d, d), lambda b, i: (0, 0)),
            pl.BlockSpec((1, tm, d), lambda b, i: (b, i, 0)),
            pl.BlockSpec((1, mblk, 3 * d),
                         (lambda b, i: (b, 0, 0)) if mrows == 1 else (lambda b, i: (b, i, 0))),
            pl.BlockSpec((1, d), lambda b, i: (0, 0)),
        ],
        out_specs=pl.BlockSpec((1, tm, d), lambda b, i: (b, i, 0)),
        compiler_params=pltpu.CompilerParams(
            dimension_semantics=("parallel", "arbitrary"), vmem_limit_bytes=VMEM_LIMIT),
        name="output_projection",
    )(yf, ys, ym, w_out, x, mod_rows, g_post)


def _layer_weights(l, g_pre, g_post, w_in, b_f, g_q_a, w_uq, g_kv_a, w_uk, w_uv, w_out):
    o = IN_OFFSETS
    w = w_in[l]

    def cols(i):
        return w[:, o[i]:o[i + 1]]

    kpe = cols(11)
    half = MLA_ROPE // 2
    zeros = lambda n: jnp.zeros((D_MODEL, n), F32)
    grp = jnp.concatenate([kpe, cols(3), zeros(64 - MLA_ROPE - 4),
                           -kpe[:, half:], kpe[:, :half], zeros(64 - MLA_ROPE)], axis=1)
    w_main = jnp.concatenate([cols(0), cols(1), cols(2), cols(4), cols(5), cols(6), cols(7), cols(8),
                              cols(9), cols(10), grp, cols(12)], axis=1).astype(BF16)
    bf_row = jnp.zeros((1, LANES), F32).at[0, MLA_ROPE:MLA_ROPE + 4].set(b_f[l])

    uq = w_uq[l].reshape(Q_LORA, MLA_HEADS, MLA_NOPE + MLA_ROPE)
    w_nope = uq[:, :, :MLA_NOPE].reshape(Q_LORA, MLA_HEADS * MLA_NOPE).astype(BF16)
    x1 = uq[:, :, MLA_NOPE:MLA_NOPE + half]
    x2 = uq[:, :, MLA_NOPE + half:]
    zpad = jnp.zeros((Q_LORA, MLA_HEADS, LANES - MLA_ROPE), F32)
    w_xy = jnp.concatenate([x1, x2, zpad, -x2, x1, zpad], axis=2)
    w_xy = jnp.transpose(w_xy, (1, 0, 2)).astype(BF16)

    ukt = jnp.transpose(w_uk[l], (1, 2, 0))
    zk = jnp.zeros_like(ukt)
    even = jnp.concatenate([ukt, zk], axis=1)
    odd = jnp.concatenate([zk, ukt], axis=1)
    is_odd = (jnp.arange(MLA_HEADS) % 2 == 1)[:, None, None]
    w_k = jnp.where(is_odd, odd, even).astype(BF16)

    uvt = jnp.transpose(w_uv[l], (1, 0, 2))
    sel = (jnp.arange(MLA_HEADS)[:, None] == jnp.arange(MLA_HEADS)[None, :]).astype(F32)
    w_v = (uvt[:, :, None, :] * sel[:, None, :, None]).reshape(MLA_HEADS, KV_LORA, MLA_WIDTH)
    return dict(g_pre=g_pre[l][None], g_post=g_post[l][None], w_main=w_main, bf_row=bf_row,
                g_q=g_q_a[l][None], w_nope=w_nope, w_xy=w_xy, w_k=w_k, g_kv=g_kv_a[l][None],
                w_v=w_v.astype(BF16), w_out=w_out[l].astype(BF16))


def _rope_tables(pos):
    half = MLA_ROPE // 2
    inv = ROPE_THETA ** (-jnp.arange(half, dtype=F32) / half)
    ang = pos.astype(F32)[:, None] * inv[None, :]
    pad = jnp.zeros((pos.shape[0], LANES - MLA_ROPE), F32)
    cos = jnp.concatenate([jnp.cos(ang), jnp.cos(ang), pad], axis=1)
    sin = jnp.concatenate([jnp.sin(ang), jnp.sin(ang), pad], axis=1)
    return cos, sin


def _pad_rows(a, total):
    pad = total - a.shape[1]
    if pad == 0:
        return a
    return jnp.concatenate([a, jnp.zeros((a.shape[0], pad) + a.shape[2:], a.dtype)], axis=1)


def _forget_cumsum(logf_all, skv):
    nb, n, _ = logf_all.shape
    length = -(-max(n, skv) // 1024) * 1024
    rows = jnp.transpose(_pad_rows(logf_all, length), (0, 2, 1)).reshape(nb * N_FS_HEADS, length)
    frow = _cumsum_rows(rows).reshape(nb, N_FS_HEADS, length)[:, :, :skv]
    return frow, jnp.transpose(frow, (0, 2, 1))


def _stream_layer(x, mod_rows, cos, sin, lw, past_rows, *, tm, past, fs_tiles, mla_tiles, batch):
    (fq_b, fk_f, fk_b, fv_f, fv_b, fg_f, sq_b, sk_f, sk_b, sv_f, sv_b, sg_f,
     qcat_b, ckv_f, kcat_b, mg_f, gout_f, fv_t, sv_t, ckv_t) = _projection(x, mod_rows, cos, sin, lw, tm)
    nb, rows, _ = x.shape
    seq = nb * rows // batch

    def per_batch(a):
        return a.reshape((batch, seq) + a.shape[2:])

    fq_b, fk_f, fk_b, fv_f, fv_b, fg_f, sq_b, sk_f, sk_b, sv_f, sv_b, sg_f, ckv_f, kcat_b, mg_f, gout_f = [
        per_batch(a) for a in (fq_b, fk_f, fk_b, fv_f, fv_b, fg_f, sq_b, sk_f, sk_b, sv_f, sv_b, sg_f,
                               ckv_f, kcat_b, mg_f, gout_f)]
    qcat_b = jnp.transpose(qcat_b.reshape(nb, MLA_HEADS, batch // nb, seq, 256),
                           (0, 2, 1, 3, 4)).reshape(batch, MLA_HEADS, seq, 256)
    kpe_f = gout_f[:, :, :MLA_ROPE]
    logf = gout_f[:, :, MLA_ROPE:MLA_ROPE + N_FS_HEADS]
    new = (fk_f, fv_f, logf, sk_f, sv_f, ckv_f, kpe_f)

    n_kv = past + seq
    tq_fs, tk_fs = fs_tiles
    tq_m, tk_m = mla_tiles
    skv = -(-n_kv // max(tk_fs, tk_m)) * max(tk_fs, tk_m)
    if past_rows is None:
        fk_a, fv_a, sk_a, sv_a, kc_a, logf_a = fk_b, fv_b, sk_b, sv_b, kcat_b, logf
    else:
        c_fk, c_fv, c_logf, c_sk, c_sv, c_ckv, c_kpe = past_rows

        def join(c, n):
            c = c.reshape(batch, past, -1).astype(BF16)
            return _pad_rows(jnp.concatenate([c, n], axis=1), skv)

        fk_a, fv_a, sk_a, sv_a = join(c_fk, fk_b), join(c_fv, fv_b), join(c_sk, sk_b), join(c_sv, sv_b)
        c_kc = jnp.concatenate([c_ckv, c_kpe, jnp.ones((batch, past, 1), F32),
                                jnp.zeros((batch, past, LANES - MLA_ROPE - 1), F32)], axis=2)
        kc_a = join(c_kc, kcat_b)
        logf_a = jnp.concatenate([c_logf, logf], axis=1)
    frow, fcol = _forget_cumsum(logf_a, skv)

    if past_rows is None:
        y_fox = _fox_attention_t(fq_b, fk_a, fv_t, fcol, frow, fg_f, tq=tq_fs, tk=tk_fs)
        y_sb = _sb_attention_t(sq_b, sk_a, sv_t, sg_f, tq=tq_fs, tk=tk_fs)
        y_mla = _mla_attention_t(qcat_b, kc_a, ckv_t, lw["w_v"], mg_f, tq=tq_m, tk=tk_m)
    else:
        y_fox = _fox_attention(fq_b, fk_a, fv_a, fcol, frow, fg_f, past=past, tq=tq_fs, tk=tk_fs)
        y_sb = _sb_attention(sq_b, sk_a, sv_a, sg_f, past=past, tq=tq_fs, tk=tk_fs)
        y_mla = _mla_attention(qcat_b, kc_a, lw["w_v"], mg_f, past=past, n_kv=n_kv, tq=tq_m, tk=tk_m)

    def per_block(a):
        return a.reshape((nb, rows) + a.shape[2:])

    x_new = _output(per_block(y_fox), per_block(y_sb), per_block(y_mla), lw["w_out"], x, mod_rows,
                    lw["g_post"], tm)
    return x_new, new


def kernel(x_prompt, x_sample, c_prompt, c_sample, cache_fox_k, cache_fox_v, cache_fox_logf, cache_sb_k, cache_sb_v, cache_mla_ckv, cache_mla_kpe, g_pre, g_post, w_ada, b_ada, w_in, b_f, g_q_a, w_uq, g_kv_a, w_uk, w_uv, w_out):
    batch, seq, d = x_prompt.shape
    dec_batch, dec_seq, _ = x_sample.shape
    past_len = cache_fox_k.shape[2]
    dec_rows = dec_batch * dec_seq

    mod = _modulation(jnp.concatenate([c_prompt, c_sample], axis=0), w_ada, b_ada)
    cos_p, sin_p = _rope_tables(jnp.arange(seq, dtype=jnp.int32))
    pos_s = past_len + (jnp.arange(dec_rows, dtype=jnp.int32) % dec_seq)
    cos_s, sin_s = _rope_tables(pos_s)
    skv_s = -(-(past_len + dec_seq) // SB_SUB) * SB_SUB

    y_p = x_prompt
    y_s = x_sample.reshape(1, dec_rows, d)
    rows_p, rows_s = [], []
    for l in range(DEPTH):
        lw = _layer_weights(l, g_pre, g_post, w_in, b_f, g_q_a, w_uq, g_kv_a, w_uk, w_uv, w_out)
        mod_p = mod[l, :batch][:, None, :]
        mod_s = jnp.repeat(mod[l, batch:], dec_seq, axis=0)[None]
        y_p, new_p = _stream_layer(y_p, mod_p, cos_p, sin_p, lw, None, tm=512, past=0,
                                   fs_tiles=(256, 256), mla_tiles=(256, 512), batch=batch)
        past_rows = (cache_fox_k[l], cache_fox_v[l], cache_fox_logf[l], cache_sb_k[l], cache_sb_v[l],
                     cache_mla_ckv[l], cache_mla_kpe[l])
        y_s, new_s = _stream_layer(y_s, mod_s, cos_s, sin_s, lw, past_rows, tm=dec_rows, past=past_len,
                                   fs_tiles=(dec_seq, skv_s), mla_tiles=(dec_seq, skv_s),
                                   batch=dec_batch)
        rows_p.append(new_p)
        rows_s.append(new_s)

    def stack(rows, idx, shape_tail):
        a = jnp.stack([r[idx] for r in rows])
        return a.reshape(a.shape[:3] + shape_tail)

    heads = (N_FS_HEADS, HEAD_DIM)
    tails = (heads, heads, (N_FS_HEADS,), heads, heads, (KV_LORA,), (MLA_ROPE,))
    outs_p = [stack(rows_p, i, t) for i, t in enumerate(tails)]
    outs_s = [stack(rows_s, i, t) for i, t in enumerate(tails)]
    return (y_p, y_s.reshape(dec_batch, dec_seq, d), *outs_p, *outs_s)
```

```python
import functools

import numpy as np
import jax
import jax.numpy as jnp
from jax import lax
from jax.experimental import pallas as pl
from jax.experimental.pallas import tpu as pltpu

D_MODEL = 1024
DEPTH = 2
CHUNK_SHIFT = 6
HEAD_DIM = 64
N_FS_HEADS = 4
FS_WIDTH = N_FS_HEADS * HEAD_DIM
MLA_HEADS = 8
MLA_GROUP = 2
MLA_NOPE = 64
MLA_ROPE = 32
MLA_V = 64
MLA_WIDTH = MLA_HEADS * MLA_V
Q_LORA = 256
KV_LORA = 128
ROPE_THETA = 10000.0
EPS = 1e-6
IN_SPLITS = (256, 256, 256, 4, 256, 256, 256, 256, 256, Q_LORA, KV_LORA, MLA_ROPE, MLA_WIDTH)
IN_OFFSETS = tuple(int(v) for v in np.cumsum((0,) + IN_SPLITS))

LANES = 128
SB_SUB = 256
CUMSUM_ROWS = 8
W_MAIN = 3072
NEG = -1e30
SB_DEAD = -110.0
FOX_DEAD = -110.0
NORM_SLACK = 1.0 + 2.0 ** -7
FS_SCALE = HEAD_DIM ** -0.5
MLA_SCALE = (MLA_NOPE + MLA_ROPE) ** -0.5
MLA_EXP2_SCALE = MLA_SCALE * float(np.log2(np.e))
ONES_LANE = KV_LORA + MLA_ROPE
ONES_ROWS = 16
VMEM_LIMIT = 56 * 1024 * 1024

BF16 = jnp.bfloat16
F32 = jnp.float32


def _dot(a, b):
    return jnp.dot(a, b, preferred_element_type=F32)


def _dot_nt(a, b):
    return lax.dot_general(a, b, (((1,), (1,)), ((), ())), preferred_element_type=F32)


def _silu(g):
    return g / (1.0 + jnp.exp(-g))


def _rms(x, g):
    return x * lax.rsqrt(jnp.mean(x * x, axis=-1, keepdims=True) + EPS) * g


def _mod_kernel(c_ref, w_ref, b_ref, o_ref):
    a = _silu(c_ref[...]).astype(BF16)
    o_ref[0] = _dot(a, w_ref[0].astype(BF16)) + b_ref[0]


def _modulation(c_all, w_ada, b_ada):
    n = c_all.shape[0]
    d = D_MODEL
    return pl.pallas_call(
        _mod_kernel,
        out_shape=jax.ShapeDtypeStruct((DEPTH, n, 3 * d), F32),
        grid=(DEPTH, 3),
        in_specs=[
            pl.BlockSpec((n, d), lambda l, j: (0, 0)),
            pl.BlockSpec((1, d, d), lambda l, j: (l, 0, j)),
            pl.BlockSpec((1, 1, d), lambda l, j: (l, 0, j)),
        ],
        out_specs=pl.BlockSpec((1, n, d), lambda l, j: (l, 0, j)),
        compiler_params=pltpu.CompilerParams(
            dimension_semantics=("arbitrary", "arbitrary"), vmem_limit_bytes=VMEM_LIMIT),
        name="modulation",
    )(c_all, w_ada, b_ada.reshape(DEPTH, 1, 3 * d))


def _proj_kernel(x_ref, mod_ref, gpre_ref, w_ref, bf_ref, gq_ref, wn_ref, wxy_ref, wk_ref,
                 gkv_ref, cos_ref, sin_ref,
                 fq_b, fk_f, fk_b, fv_f, fv_b, fg_f, sq_b, sk_f, sk_b, sv_f, sv_b, sg_f,
                 qcat_b, ckv_f, kcat_b, mg_f, gout_f, fv_t, sv_t, ckv_t):
    d = D_MODEL
    x = x_ref[0]
    mod = mod_ref[0]
    shift = mod[:, 0:d]
    scale = mod[:, d:2 * d]
    h = _rms(x, gpre_ref[...]) * (1.0 + scale) + shift
    hb = h.astype(BF16)

    def proj(a, b):
        return _dot(hb, w_ref[:, a:b])

    fq_b[0] = (proj(0, 256) * FS_SCALE).astype(BF16)
    t = proj(256, 512)
    fk_f[0] = t
    fk_b[0] = t.astype(BF16)
    t = proj(512, 768)
    fv_f[0] = t
    fv_b[0] = t.astype(BF16)
    fv_t[0] = t.T.astype(BF16)
    fg_f[0] = proj(768, 1024)
    sq_b[0] = (proj(1024, 1280) * FS_SCALE).astype(BF16)
    t = proj(1280, 1536)
    sk_f[0] = t
    sk_b[0] = t.astype(BF16)
    t = proj(1536, 1792)
    sv_f[0] = t
    sv_b[0] = t.astype(BF16)
    sv_t[0] = t.T.astype(BF16)
    sg_f[0] = proj(1792, 2048)
    mg_f[0] = proj(2560, 3072)

    cos = cos_ref[...]
    sin = sin_ref[...]

    cg = proj(2304, 2560)
    ckvn = _rms(cg[:, 0:KV_LORA], gkv_ref[...])
    ckv_f[0] = ckvn
    ckv_t[0] = jnp.concatenate([ckvn.T, jnp.ones((ONES_ROWS, ckvn.shape[0]), F32)], axis=0).astype(BF16)
    grp = cg[:, KV_LORA:2 * KV_LORA]
    rope_k = grp * cos + pltpu.roll(grp, 64, axis=1) * sin
    zf = grp + bf_ref[...]
    logf = jnp.minimum(zf, 0.0) - jnp.log(1.0 + jnp.exp(-jnp.abs(zf)))
    lane = lax.broadcasted_iota(jnp.int32, grp.shape, 1)
    gout_f[0] = jnp.where(lane < MLA_ROPE, rope_k, logf)
    ones_lane = jnp.where(lane == ONES_LANE - KV_LORA, 1.0, 0.0)
    kcat_b[0] = jnp.concatenate([ckvn, rope_k + ones_lane], axis=1).astype(BF16)

    cqn = _rms(proj(2048, 2304), gq_ref[...]).astype(BF16)
    qn = _dot(cqn, wn_ref[...]).astype(BF16)
    for hh in range(MLA_HEADS):
        pair = qn[:, (hh // 2) * LANES:(hh // 2 + 1) * LANES]
        qlat = _dot(pair, wk_ref[hh])
        xy = _dot(cqn, wxy_ref[hh])
        rope_q = xy[:, 0:LANES] * cos + xy[:, LANES:2 * LANES] * sin
        qcat_b[0, hh] = (jnp.concatenate([qlat, rope_q], axis=1) * MLA_EXP2_SCALE).astype(BF16)


def _projection(x, mod_rows, cos, sin, lw, tm):
    nb, s, d = x.shape
    mrows = mod_rows.shape[1]
    mblk = 1 if mrows == 1 else tm
    grid = (nb, s // tm)

    def row(width, dtype):
        return (jax.ShapeDtypeStruct((nb, s, width), dtype),
                pl.BlockSpec((1, tm, width), lambda b, i: (b, i, 0)))

    def col(width):
        return (jax.ShapeDtypeStruct((nb, width, s), BF16),
                pl.BlockSpec((1, width, tm), lambda b, i: (b, 0, i)))

    outs = [row(256, BF16), row(256, F32), row(256, BF16), row(256, F32), row(256, BF16),
            row(256, F32), row(256, BF16), row(256, F32), row(256, BF16), row(256, F32),
            row(256, BF16), row(256, F32),
            (jax.ShapeDtypeStruct((nb, MLA_HEADS, s, 256), BF16),
             pl.BlockSpec((1, MLA_HEADS, tm, 256), lambda b, i: (b, 0, i, 0))),
            row(KV_LORA, F32), row(256, BF16), row(MLA_WIDTH, F32), row(LANES, F32),
            col(256), col(256), col(KV_LORA + ONES_ROWS)]

    def full(a):
        nd = a.ndim
        return pl.BlockSpec(a.shape, lambda b, i: (0,) * nd)

    consts = [lw["g_pre"], lw["w_main"], lw["bf_row"], lw["g_q"], lw["w_nope"], lw["w_xy"],
              lw["w_k"], lw["g_kv"]]
    in_specs = ([pl.BlockSpec((1, tm, d), lambda b, i: (b, i, 0)),
                 pl.BlockSpec((1, mblk, 3 * d),
                              (lambda b, i: (b, 0, 0)) if mrows == 1 else (lambda b, i: (b, i, 0)))]
                + [full(a) for a in consts]
                + [pl.BlockSpec((tm, LANES), lambda b, i: (i, 0)),
                   pl.BlockSpec((tm, LANES), lambda b, i: (i, 0))])
    return pl.pallas_call(
        _proj_kernel,
        out_shape=[o[0] for o in outs],
        grid=grid,
        in_specs=in_specs,
        out_specs=[o[1] for o in outs],
        compiler_params=pltpu.CompilerParams(
            dimension_semantics=("parallel", "arbitrary"), vmem_limit_bytes=VMEM_LIMIT),
        name="projection",
    )(x, mod_rows, *consts, cos, sin)


def _cumsum_kernel(x_ref, u_ref, l_ref, o_ref):
    g, n, _ = x_ref.shape
    w_all = jnp.dot(x_ref[...].reshape(g * n, LANES), u_ref[...], preferred_element_type=F32,
                    precision=lax.Precision.HIGHEST)
    for i in range(g):
        w = w_all[i * n:(i + 1) * n]
        tot = jnp.broadcast_to(w[:, LANES - 1:LANES], w.shape)
        off = jnp.dot(l_ref[...], tot, preferred_element_type=F32, precision=lax.Precision.HIGHEST)
        o_ref[i] = w + off


def _cumsum_rows(x):
    r, length = x.shape
    n = length // LANES
    u = jnp.asarray(np.triu(np.ones((LANES, LANES), np.float32)))
    lo = jnp.asarray(np.tril(np.ones((n, n), np.float32), -1))
    out = pl.pallas_call(
        _cumsum_kernel,
        out_shape=jax.ShapeDtypeStruct((r, n, LANES), F32),
        grid=(r // CUMSUM_ROWS,),
        in_specs=[pl.BlockSpec((CUMSUM_ROWS, n, LANES), lambda i: (i, 0, 0)),
                  pl.BlockSpec((LANES, LANES), lambda i: (0, 0)),
                  pl.BlockSpec((n, n), lambda i: (0, 0))],
        out_specs=pl.BlockSpec((CUMSUM_ROWS, n, LANES), lambda i: (i, 0, 0)),
        compiler_params=pltpu.CompilerParams(dimension_semantics=("arbitrary",)),
        name="cumsum_logf",
    )(x.reshape(r, n, LANES), u, lo)
    return out.reshape(r, length)


def _schedule(nq, kmax_fn, reverse):
    qi_l, kj_l, first_l, last_l = [], [], [], []
    for qi in range(nq):
        ks = list(range(kmax_fn(qi) + 1))
        if reverse:
            ks = ks[::-1]
        for n, kj in enumerate(ks):
            qi_l.append(qi)
            kj_l.append(kj)
            first_l.append(int(n == 0))
            last_l.append(int(n == len(ks) - 1))
    return tuple(jnp.asarray(np.asarray(a, np.int32)) for a in (qi_l, kj_l, first_l, last_l))


def _head_lane_mask(shape, h):
    lane = lax.broadcasted_iota(jnp.int32, shape, 1)
    return (lane >= HEAD_DIM * h) & (lane < HEAD_DIM * (h + 1))


def _fox_kernel(qt, kt, ft, lt, q_ref, k_ref, v_ref, fc_ref, fr_ref, g_ref, o_ref,
                qm_sc, m_sc, l_sc, acc_sc, *, past, tq, tk):
    step = pl.program_id(1)
    qi = qt[step]
    kj = kt[step]

    @pl.when(ft[step] == 1)
    def _():
        q = q_ref[0]
        for h in range(N_FS_HEADS):
            qm_sc[h] = jnp.where(_head_lane_mask(q.shape, h), q, jnp.zeros_like(q))
        m_sc[...] = jnp.full(m_sc.shape, NEG, F32)
        l_sc[...] = jnp.zeros(l_sc.shape, F32)
        acc_sc[...] = jnp.zeros(acc_sc.shape, F32)

    k = k_ref[0]
    v = v_ref[0]
    q_pos = past + qi * tq + lax.broadcasted_iota(jnp.int32, (tq, 1), 0)
    k_pos = kj * tk + lax.broadcasted_iota(jnp.int32, (1, tk), 1)
    mask = k_pos <= q_pos
    fc = fc_ref[0]
    fr = fr_ref[0]
    for h in range(N_FS_HEADS):
        s = _dot_nt(qm_sc[h], k)
        s = s + (fc[:, h:h + 1] - fr[h:h + 1, :])
        s = jnp.where(mask, s, NEG)
        m_prev = m_sc[h]
        m_new = jnp.maximum(m_prev, jnp.max(s, axis=1, keepdims=True))
        p = jnp.exp(s - m_new)
        alpha = jnp.exp(m_prev - m_new)
        l_sc[h] = alpha * l_sc[h] + jnp.sum(p, axis=1, keepdims=True)
        acc_sc[h] = alpha * acc_sc[h] + _dot(p.astype(BF16), v)
        m_sc[h] = m_new

    @pl.when(lt[step] == 1)
    def _():
        o = jnp.zeros((tq, FS_WIDTH), F32)
        for h in range(N_FS_HEADS):
            o = jnp.where(_head_lane_mask(o.shape, h), acc_sc[h] / l_sc[h], o)
        o_ref[0] = (o * _silu(g_ref[0])).astype(BF16)


def _fox_attention(q, k, v, fcol, frow, g, *, past, tq, tk):
    nb, sq, _ = q.shape
    nq = sq // tq
    tabs = _schedule(nq, lambda qi: (past + (qi + 1) * tq - 1) // tk, False)
    nsteps = int(tabs[0].shape[0])
    qoff = past // tq
    kern = functools.partial(_fox_kernel, past=past, tq=tq, tk=tk)
    return pl.pallas_call(
        kern,
        out_shape=jax.ShapeDtypeStruct((nb, sq, FS_WIDTH), BF16),
        grid_spec=pltpu.PrefetchScalarGridSpec(
            num_scalar_prefetch=4,
            grid=(nb, nsteps),
            in_specs=[
                pl.BlockSpec((1, tq, FS_WIDTH), lambda b, s, qt, kt, ft, lt: (b, qt[s], 0)),
                pl.BlockSpec((1, tk, FS_WIDTH), lambda b, s, qt, kt, ft, lt: (b, kt[s], 0)),
                pl.BlockSpec((1, tk, FS_WIDTH), lambda b, s, qt, kt, ft, lt: (b, kt[s], 0)),
                pl.BlockSpec((1, tq, N_FS_HEADS), lambda b, s, qt, kt, ft, lt: (b, qoff + qt[s], 0)),
                pl.BlockSpec((1, N_FS_HEADS, tk), lambda b, s, qt, kt, ft, lt: (b, 0, kt[s])),
                pl.BlockSpec((1, tq, FS_WIDTH), lambda b, s, qt, kt, ft, lt: (b, qt[s], 0)),
            ],
            out_specs=pl.BlockSpec((1, tq, FS_WIDTH), lambda b, s, qt, kt, ft, lt: (b, qt[s], 0)),
            scratch_shapes=[
                pltpu.VMEM((N_FS_HEADS, tq, FS_WIDTH), BF16),
                pltpu.VMEM((N_FS_HEADS, tq, 1), F32),
                pltpu.VMEM((N_FS_HEADS, tq, 1), F32),
                pltpu.VMEM((N_FS_HEADS, tq, FS_WIDTH), F32),
            ],
        ),
        compiler_params=pltpu.CompilerParams(
            dimension_semantics=("parallel", "arbitrary"), vmem_limit_bytes=VMEM_LIMIT),
        name="fox_attention",
    )(*tabs, q, k, v, fcol, frow, g)


def _sb_kernel(qt, kt, ft, lt, q_ref, k_ref, v_ref, tri_ref, g_ref, o_ref,
               qm_sc, r_sc, acc_sc, *, past, tq, tk):
    step = pl.program_id(1)
    qi = qt[step]
    kj = kt[step]

    @pl.when(ft[step] == 1)
    def _():
        q = q_ref[0]
        for h in range(N_FS_HEADS):
            qm_sc[h] = jnp.where(_head_lane_mask(q.shape, h), q, jnp.zeros_like(q))
        r_sc[...] = jnp.zeros(r_sc.shape, F32)
        acc_sc[...] = jnp.zeros(acc_sc.shape, F32)

    q_pos = past + qi * tq + lax.broadcasted_iota(jnp.int32, (tq, 1), 0)

    def sub_block(sub):
        tri = tri_ref[...]
        k = k_ref[0, sub * SB_SUB:(sub + 1) * SB_SUB, :]
        v = v_ref[0, sub * SB_SUB:(sub + 1) * SB_SUB, :]
        k_pos = kj * tk + sub * SB_SUB + lax.broadcasted_iota(jnp.int32, (1, SB_SUB), 1)
        mask = k_pos < q_pos
        for h in range(N_FS_HEADS):
            z = _dot_nt(qm_sc[h], k)
            sp = jnp.maximum(z, 0.0) + jnp.log(1.0 + jnp.exp(-jnp.abs(z)))
            l1m = jnp.where(mask, -sp, 0.0)
            hi = l1m.astype(BF16)
            lo = (l1m - hi.astype(F32)).astype(BF16)
            c = _dot(hi, tri) + _dot(lo, tri)
            r = r_sc[h]
            a = jnp.where(mask, jnp.exp(z + (c + r)), 0.0)
            acc_sc[h] = acc_sc[h] + _dot(a.astype(BF16), v)
            r_sc[h] = r + c[:, 0:1]

    for sub in reversed(range(tk // SB_SUB)):
        pl.when(jnp.max(r_sc[...]) > SB_DEAD)(functools.partial(sub_block, sub))

    @pl.when(lt[step] == 1)
    def _():
        o = jnp.zeros((tq, FS_WIDTH), F32)
        for h in range(N_FS_HEADS):
            o = jnp.where(_head_lane_mask(o.shape, h), acc_sc[h], o)
        o_ref[0] = (o * _silu(g_ref[0])).astype(BF16)


def _sb_attention(q, k, v, g, *, past, tq, tk):
    nb, sq, _ = q.shape
    nq = sq // tq
    tabs = _schedule(nq, lambda qi: (past + (qi + 1) * tq - 1) // tk, True)
    nsteps = int(tabs[0].shape[0])
    tri = jnp.asarray(np.tril(np.ones((SB_SUB, SB_SUB), np.float32))).astype(BF16)
    kern = functools.partial(_sb_kernel, past=past, tq=tq, tk=tk)
    return pl.pallas_call(
        kern,
        out_shape=jax.ShapeDtypeStruct((nb, sq, FS_WIDTH), BF16),
        grid_spec=pltpu.PrefetchScalarGridSpec(
            num_scalar_prefetch=4,
            grid=(nb, nsteps),
            in_specs=[
                pl.BlockSpec((1, tq, FS_WIDTH), lambda b, s, qt, kt, ft, lt: (b, qt[s], 0)),
                pl.BlockSpec((1, tk, FS_WIDTH), lambda b, s, qt, kt, ft, lt: (b, kt[s], 0)),
                pl.BlockSpec((1, tk, FS_WIDTH), lambda b, s, qt, kt, ft, lt: (b, kt[s], 0)),
                pl.BlockSpec((SB_SUB, SB_SUB), lambda b, s, qt, kt, ft, lt: (0, 0)),
                pl.BlockSpec((1, tq, FS_WIDTH), lambda b, s, qt, kt, ft, lt: (b, qt[s], 0)),
            ],
            out_specs=pl.BlockSpec((1, tq, FS_WIDTH), lambda b, s, qt, kt, ft, lt: (b, qt[s], 0)),
            scratch_shapes=[
                pltpu.VMEM((N_FS_HEADS, tq, FS_WIDTH), BF16),
                pltpu.VMEM((N_FS_HEADS, tq, 1), F32),
                pltpu.VMEM((N_FS_HEADS, tq, FS_WIDTH), F32),
            ],
        ),
        compiler_params=pltpu.CompilerParams(
            dimension_semantics=("parallel", "arbitrary"), vmem_limit_bytes=VMEM_LIMIT),
        name="sb_attention",
    )(*tabs, q, k, v, tri, g)


def _mla_kernel(qt, kt, ft, lt, q_ref, k_ref, wv_ref, g_ref, o_ref,
                m_sc, acc_sc, *, past, n_kv, tq, tk):
    step = pl.program_id(1)
    qi = qt[step]
    kj = kt[step]
    rows = MLA_HEADS * tq

    @pl.when(ft[step] == 1)
    def _():
        m_sc[...] = jnp.full(m_sc.shape, NEG, F32)
        acc_sc[...] = jnp.zeros(acc_sc.shape, F32)

    k = k_ref[0]
    q_pos = past + qi * tq + lax.broadcasted_iota(jnp.int32, (tq, 1), 0)
    k_pos = kj * tk + lax.broadcasted_iota(jnp.int32, (1, tk), 1)
    mask = ((k_pos >> CHUNK_SHIFT) <= (q_pos >> CHUNK_SHIFT)) & (k_pos < n_kv)
    s = _dot_nt(q_ref[0].reshape(rows, 256), k).reshape(MLA_HEADS, tq, tk)
    s = jnp.where(mask[None], s, NEG)
    m_prev = m_sc[...]
    m_new = jnp.maximum(m_prev, jnp.max(s, axis=2, keepdims=True))
    p = jnp.exp2(s - m_new)
    alpha = jnp.exp2(m_prev - m_new)
    pv = _dot(p.reshape(rows, tk).astype(BF16), k)
    acc_sc[...] = alpha.reshape(rows, 1) * acc_sc[...] + pv
    m_sc[...] = m_new

    @pl.when(lt[step] == 1)
    def _():
        acc = acc_sc[...]
        o_lat = (acc[:, 0:KV_LORA] / acc[:, ONES_LANE:ONES_LANE + 1]).astype(BF16)
        o = jnp.zeros((tq, MLA_WIDTH), F32)
        for h in range(MLA_HEADS):
            o = o + _dot(o_lat[h * tq:(h + 1) * tq], wv_ref[h])
        o_ref[0] = (o * _silu(g_ref[0])).astype(BF16)


def _mla_attention(qcat, kcat, wv, g, *, past, n_kv, tq, tk):
    nb, _, sq, _ = qcat.shape
    nq = sq // tq

    def kmax(qi):
        last_q = past + (qi + 1) * tq - 1
        end = min(((last_q >> CHUNK_SHIFT) + 1) << CHUNK_SHIFT, n_kv)
        return (end - 1) // tk

    tabs = _schedule(nq, kmax, False)
    nsteps = int(tabs[0].shape[0])
    kern = functools.partial(_mla_kernel, past=past, n_kv=n_kv, tq=tq, tk=tk)
    return pl.pallas_call(
        kern,
        out_shape=jax.ShapeDtypeStruct((nb, sq, MLA_WIDTH), BF16),
        grid_spec=pltpu.PrefetchScalarGridSpec(
            num_scalar_prefetch=4,
            grid=(nb, nsteps),
            in_specs=[
                pl.BlockSpec((1, MLA_HEADS, tq, 256), lambda b, s, qt, kt, ft, lt: (b, 0, qt[s], 0)),
                pl.BlockSpec((1, tk, 256), lambda b, s, qt, kt, ft, lt: (b, kt[s], 0)),
                pl.BlockSpec((MLA_HEADS, KV_LORA, MLA_WIDTH), lambda b, s, qt, kt, ft, lt: (0, 0, 0)),
                pl.BlockSpec((1, tq, MLA_WIDTH), lambda b, s, qt, kt, ft, lt: (b, qt[s], 0)),
            ],
            out_specs=pl.BlockSpec((1, tq, MLA_WIDTH), lambda b, s, qt, kt, ft, lt: (b, qt[s], 0)),
            scratch_shapes=[
                pltpu.VMEM((MLA_HEADS, tq, 1), F32),
                pltpu.VMEM((MLA_HEADS * tq, 256), F32),
            ],
        ),
        compiler_params=pltpu.CompilerParams(
            dimension_semantics=("parallel", "arbitrary"), vmem_limit_bytes=VMEM_LIMIT),
        name="mla_attention",
    )(*tabs, qcat, kcat, wv, g)


def _stack_heads(q, qm_sc, tq):
    for h in range(N_FS_HEADS):
        qm_sc[h * tq:(h + 1) * tq, :] = jnp.where(_head_lane_mask(q.shape, h), q, jnp.zeros_like(q))


def _query_positions(past, qi, tq, heads):
    lane = lax.broadcasted_iota(jnp.int32, (1, heads * tq), 1)
    return past + qi * tq + (lane & (tq - 1))


def _log2(n):
    assert n & (n - 1) == 0
    return n.bit_length() - 1


def _live_flag(j, value, floor):
    return jnp.logical_and(j >= 0, value > floor).astype(jnp.int32)


def _fox_t_kernel(q_ref, k_ref, vt_ref, fc_ref, fr_ref, g_ref, e_ref, o_ref,
                  qm_sc, fkb_sc, pm_sc, fq_sc, qn_sc, m_sc, l_sc, acc_sc, *, tq, tk):
    qi = pl.program_id(1)
    seq = k_ref.shape[1]
    groups = tq // LANES

    def per_head_row(x):
        return jnp.concatenate([x[:, (g // groups) * LANES:(g // groups + 1) * LANES]
                                for g in range(N_FS_HEADS * groups)], axis=1)

    @pl.when(qi == 0)
    def _():
        fc = fc_ref[0]
        for h in range(N_FS_HEADS):
            fkb_sc[h] = jnp.broadcast_to(fc[:, h:h + 1], (seq, LANES))
        running = jnp.zeros((1, N_FS_HEADS * LANES), F32)
        for j in range(seq // tk):
            kf = k_ref[0, j * tk:(j + 1) * tk, :].astype(F32)
            n2 = _dot((kf * kf).astype(BF16), e_ref[...])
            running = jnp.maximum(running, jnp.max(n2, axis=0, keepdims=True))
            pm_sc[j:j + 1, :] = running

    _stack_heads(q_ref[0], qm_sc, tq)
    fr = fr_ref[0]
    fq_sc[...] = jnp.concatenate([fr[h:h + 1, :] for h in range(N_FS_HEADS)], axis=1)
    qf = qm_sc[...].astype(F32)
    qn_sc[...] = _dot_nt(jnp.ones((8, FS_WIDTH), BF16), (qf * qf).astype(BF16))[0:1, :]
    m_sc[...] = jnp.full(m_sc.shape, NEG, F32)
    l_sc[...] = jnp.zeros(l_sc.shape, F32)
    acc_sc[...] = jnp.zeros(acc_sc.shape, F32)

    def tile(j, masked):
        start = pl.multiple_of(j * tk, tk)
        s_t = _dot_nt(k_ref[0, pl.ds(start, tk), :], qm_sc[...])
        fk = fkb_sc[:, pl.ds(start, tk), :]
        s_t = jnp.concatenate(
            [s_t[:, g * LANES:(g + 1) * LANES] - fk[g // groups] for g in range(N_FS_HEADS * groups)],
            axis=1) + fq_sc[...]
        if masked:
            k_pos = j * tk + lax.broadcasted_iota(jnp.int32, (tk, 1), 0)
            s_t = jnp.where(k_pos <= _query_positions(0, qi, tq, N_FS_HEADS), s_t, NEG)
        m_prev = m_sc[...]
        m_new = jnp.maximum(m_prev, jnp.max(s_t, axis=0, keepdims=True))
        p_t = jnp.exp(s_t - m_new)
        alpha = jnp.exp(m_prev - m_new)
        l_sc[...] = alpha * l_sc[...] + jnp.sum(p_t, axis=0, keepdims=True)
        m_sc[...] = m_new
        p_b = p_t.astype(BF16)
        for h in range(N_FS_HEADS):
            pv = _dot(vt_ref[0, h * HEAD_DIM:(h + 1) * HEAD_DIM, pl.ds(start, tk)],
                      p_b[:, h * tq:(h + 1) * tq])
            acc_sc[h] = alpha[:, h * tq:(h + 1) * tq] * acc_sc[h] + pv

    def live(j):
        jc = jnp.maximum(j, 0)
        pm = per_head_row(pm_sc[pl.ds(jc, 1), :])
        fend = per_head_row(jnp.concatenate(
            [fkb_sc[h, pl.ds((jc + 1) * tk - 1, 1), :] for h in range(N_FS_HEADS)], axis=1))
        bound = jnp.sqrt(qn_sc[...] * pm) * NORM_SLACK + fq_sc[...] - fend - m_sc[...]
        return _live_flag(j, jnp.max(bound), FOX_DEAD)

    j_diag = ((qi + 1) * tq - 1) >> _log2(tk)
    tile(j_diag, True)

    def step(carry):
        j = carry[0]
        tile(j, False)
        return j - 1, live(j - 1)

    lax.while_loop(lambda c: c[1] > 0, step, (j_diag - 1, live(j_diag - 1)))

    l = l_sc[...]
    o_t = jnp.concatenate([acc_sc[h] / l[:, h * tq:(h + 1) * tq] for h in range(N_FS_HEADS)], axis=0)
    o_ref[0] = (o_t.T * _silu(g_ref[0])).astype(BF16)


def _fox_attention_t(q, k, vt, fcol, frow, g, *, tq, tk):
    nb, sq, _ = q.shape
    rows = N_FS_HEADS * tq
    head_of_col = np.arange(FS_WIDTH)[:, None] // HEAD_DIM
    head_of_lane = np.arange(N_FS_HEADS * LANES)[None, :] // LANES
    expand = jnp.asarray((head_of_col == head_of_lane).astype(np.float32)).astype(BF16)
    kern = functools.partial(_fox_t_kernel, tq=tq, tk=tk)
    return pl.pallas_call(
        kern,
        out_shape=jax.ShapeDtypeStruct((nb, sq, FS_WIDTH), BF16),
        grid=(nb, sq // tq),
        in_specs=[
            pl.BlockSpec((1, tq, FS_WIDTH), lambda b, i: (b, i, 0)),
            pl.BlockSpec((1, sq, FS_WIDTH), lambda b, i: (b, 0, 0)),
            pl.BlockSpec((1, FS_WIDTH, sq), lambda b, i: (b, 0, 0)),
            pl.BlockSpec((1, sq, N_FS_HEADS), lambda b, i: (b, 0, 0)),
            pl.BlockSpec((1, N_FS_HEADS, tq), lambda b, i: (b, 0, i)),
            pl.BlockSpec((1, tq, FS_WIDTH), lambda b, i: (b, i, 0)),
            pl.BlockSpec(expand.shape, lambda b, i: (0, 0)),
        ],
        out_specs=pl.BlockSpec((1, tq, FS_WIDTH), lambda b, i: (b, i, 0)),
        scratch_shapes=[
            pltpu.VMEM((rows, FS_WIDTH), BF16),
            pltpu.VMEM((N_FS_HEADS, sq, LANES), F32),
            pltpu.VMEM((sq // tk, N_FS_HEADS * LANES), F32),
            pltpu.VMEM((1, rows), F32),
            pltpu.VMEM((1, rows), F32),
            pltpu.VMEM((1, rows), F32),
            pltpu.VMEM((1, rows), F32),
            pltpu.VMEM((N_FS_HEADS, HEAD_DIM, tq), F32),
        ],
        compiler_params=pltpu.CompilerParams(
            dimension_semantics=("parallel", "arbitrary"), vmem_limit_bytes=VMEM_LIMIT),
        name="fox_attention_t",
    )(q, k, vt, fcol, frow, g, expand)


def _sb_t_kernel(q_ref, k_ref, vt_ref, tri_ref, g_ref, o_ref, qm_sc, r_sc, acc_sc, *, tq, tk):
    qi = pl.program_id(1)
    _stack_heads(q_ref[0], qm_sc, tq)
    r_sc[...] = jnp.zeros(r_sc.shape, F32)
    acc_sc[...] = jnp.zeros(acc_sc.shape, F32)

    def tile(j, masked):
        start = pl.multiple_of(j * tk, tk)
        z_all = _dot_nt(k_ref[0, pl.ds(start, tk), :], qm_sc[...])
        tri_t = tri_ref[...]
        for sub in reversed(range(tk // SB_SUB)):
            z = z_all[sub * SB_SUB:(sub + 1) * SB_SUB, :]
            l1m = -(jnp.maximum(z, 0.0) + jnp.log(1.0 + jnp.exp(-jnp.abs(z))))
            if masked:
                k_pos = j * tk + sub * SB_SUB + lax.broadcasted_iota(jnp.int32, (SB_SUB, 1), 0)
                mask = k_pos < _query_positions(0, qi, tq, N_FS_HEADS)
                l1m = jnp.where(mask, l1m, 0.0)
            hi = l1m.astype(BF16)
            lo = (l1m - hi.astype(F32)).astype(BF16)
            c = _dot(tri_t, hi) + _dot(tri_t, lo)
            r = r_sc[...]
            a = jnp.exp(z + (c + r))
            if masked:
                a = jnp.where(mask, a, 0.0)
            a_b = a.astype(BF16)
            for h in range(N_FS_HEADS):
                acc_sc[h] = acc_sc[h] + _dot(
                    vt_ref[0, h * HEAD_DIM:(h + 1) * HEAD_DIM,
                           pl.ds(pl.multiple_of(start + sub * SB_SUB, SB_SUB), SB_SUB)],
                    a_b[:, h * tq:(h + 1) * tq])
            r_sc[...] = r + c[0:1, :]

    def live(j):
        return _live_flag(j, jnp.max(r_sc[...]), SB_DEAD)

    j_diag = ((qi + 1) * tq - 1) >> _log2(tk)
    tile(j_diag, True)

    def step(carry):
        j = carry[0]
        tile(j, False)
        return j - 1, live(j - 1)

    lax.while_loop(lambda c: c[1] > 0, step, (j_diag - 1, live(j_diag - 1)))

    o_t = jnp.concatenate([acc_sc[h] for h in range(N_FS_HEADS)], axis=0)
    o_ref[0] = (o_t.T * _silu(g_ref[0])).astype(BF16)


def _sb_attention_t(q, k, vt, g, *, tq, tk):
    nb, sq, _ = q.shape
    rows = N_FS_HEADS * tq
    tri_t = jnp.asarray(np.triu(np.ones((SB_SUB, SB_SUB), np.float32))).astype(BF16)
    kern = functools.partial(_sb_t_kernel, tq=tq, tk=tk)
    return pl.pallas_call(
        kern,
        out_shape=jax.ShapeDtypeStruct((nb, sq, FS_WIDTH), BF16),
        grid=(nb, sq // tq),
        in_specs=[
            pl.BlockSpec((1, tq, FS_WIDTH), lambda b, i: (b, i, 0)),
            pl.BlockSpec((1, sq, FS_WIDTH), lambda b, i: (b, 0, 0)),
            pl.BlockSpec((1, FS_WIDTH, sq), lambda b, i: (b, 0, 0)),
            pl.BlockSpec((SB_SUB, SB_SUB), lambda b, i: (0, 0)),
            pl.BlockSpec((1, tq, FS_WIDTH), lambda b, i: (b, i, 0)),
        ],
        out_specs=pl.BlockSpec((1, tq, FS_WIDTH), lambda b, i: (b, i, 0)),
        scratch_shapes=[
            pltpu.VMEM((rows, FS_WIDTH), BF16),
            pltpu.VMEM((1, rows), F32),
            pltpu.VMEM((N_FS_HEADS, HEAD_DIM, tq), F32),
        ],
        compiler_params=pltpu.CompilerParams(
            dimension_semantics=("parallel", "arbitrary"), vmem_limit_bytes=VMEM_LIMIT),
        name="sb_attention_t",
    )(q, k, vt, tri_t, g)


def _mla_t_kernel(q_ref, k_ref, vt_ref, wv_ref, g_ref, o_ref, m_sc, acc_sc, *, tq, tk):
    qi = pl.program_id(1)
    width = MLA_GROUP * tq
    n_groups = MLA_HEADS // MLA_GROUP
    m_sc[...] = jnp.full(m_sc.shape, NEG, F32)
    acc_sc[...] = jnp.zeros(acc_sc.shape, F32)

    def tile(j, masked, keys=tk):
        start = pl.multiple_of(j * tk, tk)
        k = k_ref[0, pl.ds(start, keys), :]
        vt = vt_ref[0, :, pl.ds(start, keys)]
        if masked:
            k_pos = j * tk + lax.broadcasted_iota(jnp.int32, (keys, 1), 0)
            q_pos = _query_positions(0, qi, tq, MLA_GROUP)
            mask = (k_pos >> CHUNK_SHIFT) <= (q_pos >> CHUNK_SHIFT)

        def scores(g):
            return _dot_nt(k, q_ref[0, g * MLA_GROUP:(g + 1) * MLA_GROUP].reshape(width, 256))

        def accumulate(g, alpha, p_b):
            cols = slice(g * width, (g + 1) * width)
            acc_sc[:, cols] = alpha * acc_sc[:, cols] + _dot(vt, p_b)

        s_next = scores(0)
        pending = None
        for g in range(n_groups):
            s_t = s_next
            if g + 1 < n_groups:
                s_next = scores(g + 1)
            cols = slice(g * width, (g + 1) * width)
            if masked:
                s_t = jnp.where(mask, s_t, NEG)
            m_prev = m_sc[:, cols]
            m_new = jnp.maximum(m_prev, jnp.max(s_t, axis=0, keepdims=True))
            p_b = jnp.exp2(s_t - m_new).astype(BF16)
            alpha = jnp.exp2(m_prev - m_new)
            m_sc[:, cols] = m_new
            if pending is not None:
                accumulate(*pending)
            pending = (g, alpha, p_b)
        accumulate(*pending)

    visible_end = ((((qi + 1) * tq - 1) >> CHUNK_SHIFT) + 1) << CHUNK_SHIFT
    j_last = (visible_end - 1) >> _log2(tk)

    def step(j, carry):
        tile(j, False)
        return carry

    lax.fori_loop(0, j_last, step, 0)
    short = visible_end - j_last * tk <= tk // 2
    pl.when(short)(lambda: tile(j_last, True, tk // 2))
    pl.when(jnp.logical_not(short))(lambda: tile(j_last, True))

    o_lat_t = acc_sc[0:KV_LORA, :] / acc_sc[KV_LORA:KV_LORA + 1, :]
    o = jnp.zeros((tq, MLA_WIDTH), F32)
    for h in range(MLA_HEADS):
        o = o + _dot(o_lat_t[:, h * tq:(h + 1) * tq].T.astype(BF16), wv_ref[h])
    o_ref[0] = (o * _silu(g_ref[0])).astype(BF16)


def _mla_attention_t(qcat, kcat, ckv_t, wv, g, *, tq, tk):
    nb, _, sq, _ = qcat.shape
    rows = MLA_HEADS * tq
    kern = functools.partial(_mla_t_kernel, tq=tq, tk=tk)
    return pl.pallas_call(
        kern,
        out_shape=jax.ShapeDtypeStruct((nb, sq, MLA_WIDTH), BF16),
        grid=(nb, sq // tq),
        in_specs=[
            pl.BlockSpec((1, MLA_HEADS, tq, 256), lambda b, i: (b, 0, i, 0)),
            pl.BlockSpec((1, sq, 256), lambda b, i: (b, 0, 0)),
            pl.BlockSpec((1, KV_LORA + ONES_ROWS, sq), lambda b, i: (b, 0, 0)),
            pl.BlockSpec((MLA_HEADS, KV_LORA, MLA_WIDTH), lambda b, i: (0, 0, 0)),
            pl.BlockSpec((1, tq, MLA_WIDTH), lambda b, i: (b, i, 0)),
        ],
        out_specs=pl.BlockSpec((1, tq, MLA_WIDTH), lambda b, i: (b, i, 0)),
        scratch_shapes=[
            pltpu.VMEM((1, rows), F32),
            pltpu.VMEM((KV_LORA + ONES_ROWS, rows), F32),
        ],
        compiler_params=pltpu.CompilerParams(
            dimension_semantics=("parallel", "arbitrary"), vmem_limit_bytes=VMEM_LIMIT),
        name="mla_attention_t",
    )(qcat, kcat, ckv_t, wv, g)


def _out_kernel(yf_ref, ys_ref, ym_ref, w_ref, x_ref, mod_ref, gpost_ref, o_ref):
    d = D_MODEL
    y = (_dot(yf_ref[0], w_ref[0:256, :]) + _dot(ys_ref[0], w_ref[256:512, :])
         + _dot(ym_ref[0], w_ref[512:1024, :]))
    gate = mod_ref[0][:, 2 * d:3 * d]
    o_ref[0] = x_ref[0] + gate * _rms(y, gpost_ref[...])


def _output(yf, ys, ym, w_out, x, mod_rows, g_post, tm):
    nb, s, d = x.shape
    mrows = mod_rows.shape[1]
    mblk = 1 if mrows == 1 else tm
    return pl.pallas_call(
        _out_kernel,
        out_shape=jax.ShapeDtypeStruct((nb, s, d), F32),
        grid=(nb, s // tm),
        in_specs=[
            pl.BlockSpec((1, tm, 256), lambda b, i: (b, i, 0)),
            pl.BlockSpec((1, tm, 256), lambda b, i: (b, i, 0)),
            pl.BlockSpec((1, tm, 512), lambda b, i: (b, i, 0)),
            pl.BlockSpec((d, d), lambda b, i: (0, 0)),
            pl.BlockSpec((1, tm, d), lambda b, i: (b, i, 0)),
            pl.BlockSpec((1, mblk, 3 * d),
                         (lambda b, i: (b, 0, 0)) if mrows == 1 else (lambda b, i: (b, i, 0))),
            pl.BlockSpec((1, d), lambda b, i: (0, 0)),
        ],
        out_specs=pl.BlockSpec((1, tm, d), lambda b, i: (b, i, 0)),
        compiler_params=pltpu.CompilerParams(
            dimension_semantics=("parallel", "arbitrary"), vmem_limit_bytes=VMEM_LIMIT),
        name="output_projection",
    )(yf, ys, ym, w_out, x, mod_rows, g_post)


def _layer_weights(l, g_pre, g_post, w_in, b_f, g_q_a, w_uq, g_kv_a, w_uk, w_uv, w_out):
    o = IN_OFFSETS
    w = w_in[l]

    def cols(i):
        return w[:, o[i]:o[i + 1]]

    kpe = cols(11)
    half = MLA_ROPE // 2
    zeros = lambda n: jnp.zeros((D_MODEL, n), F32)
    grp = jnp.concatenate([kpe, cols(3), zeros(64 - MLA_ROPE - 4),
                           -kpe[:, half:], kpe[:, :half], zeros(64 - MLA_ROPE)], axis=1)
    w_main = jnp.concatenate([cols(0), cols(1), cols(2), cols(4), cols(5), cols(6), cols(7), cols(8),
                              cols(9), cols(10), grp, cols(12)], axis=1).astype(BF16)
    bf_row = jnp.zeros((1, LANES), F32).at[0, MLA_ROPE:MLA_ROPE + 4].set(b_f[l])

    uq = w_uq[l].reshape(Q_LORA, MLA_HEADS, MLA_NOPE + MLA_ROPE)
    w_nope = uq[:, :, :MLA_NOPE].reshape(Q_LORA, MLA_HEADS * MLA_NOPE).astype(BF16)
    x1 = uq[:, :, MLA_NOPE:MLA_NOPE + half]
    x2 = uq[:, :, MLA_NOPE + half:]
    zpad = jnp.zeros((Q_LORA, MLA_HEADS, LANES - MLA_ROPE), F32)
    w_xy = jnp.concatenate([x1, x2, zpad, -x2, x1, zpad], axis=2)
    w_xy = jnp.transpose(w_xy, (1, 0, 2)).astype(BF16)

    ukt = jnp.transpose(w_uk[l], (1, 2, 0))
    zk = jnp.zeros_like(ukt)
    even = jnp.concatenate([ukt, zk], axis=1)
    odd = jnp.concatenate([zk, ukt], axis=1)
    is_odd = (jnp.arange(MLA_HEADS) % 2 == 1)[:, None, None]
    w_k = jnp.where(is_odd, odd, even).astype(BF16)

    uvt = jnp.transpose(w_uv[l], (1, 0, 2))
    sel = (jnp.arange(MLA_HEADS)[:, None] == jnp.arange(MLA_HEADS)[None, :]).astype(F32)
    w_v = (uvt[:, :, None, :] * sel[:, None, :, None]).reshape(MLA_HEADS, KV_LORA, MLA_WIDTH)
    return dict(g_pre=g_pre[l][None], g_post=g_post[l][None], w_main=w_main, bf_row=bf_row,
                g_q=g_q_a[l][None], w_nope=w_nope, w_xy=w_xy, w_k=w_k, g_kv=g_kv_a[l][None],
                w_v=w_v.astype(BF16), w_out=w_out[l].astype(BF16))


def _rope_tables(pos):
    half = MLA_ROPE // 2
    inv = ROPE_THETA ** (-jnp.arange(half, dtype=F32) / half)
    ang = pos.astype(F32)[:, None] * inv[None, :]
    pad = jnp.zeros((pos.shape[0], LANES - MLA_ROPE), F32)
    cos = jnp.concatenate([jnp.cos(ang), jnp.cos(ang), pad], axis=1)
    sin = jnp.concatenate([jnp.sin(ang), jnp.sin(ang), pad], axis=1)
    return cos, sin


def _pad_rows(a, total):
    pad = total - a.shape[1]
    if pad == 0:
        return a
    return jnp.concatenate([a, jnp.zeros((a.shape[0], pad) + a.shape[2:], a.dtype)], axis=1)


def _forget_cumsum(logf_all, skv):
    nb, n, _ = logf_all.shape
    length = -(-max(n, skv) // 1024) * 1024
    rows = jnp.transpose(_pad_rows(logf_all, length), (0, 2, 1)).reshape(nb * N_FS_HEADS, length)
    frow = _cumsum_rows(rows).reshape(nb, N_FS_HEADS, length)[:, :, :skv]
    return frow, jnp.transpose(frow, (0, 2, 1))


def _stream_layer(x, mod_rows, cos, sin, lw, past_rows, *, tm, past, fs_tiles, mla_tiles, batch):
    (fq_b, fk_f, fk_b, fv_f, fv_b, fg_f, sq_b, sk_f, sk_b, sv_f, sv_b, sg_f,
     qcat_b, ckv_f, kcat_b, mg_f, gout_f, fv_t, sv_t, ckv_t) = _projection(x, mod_rows, cos, sin, lw, tm)
    nb, rows, _ = x.shape
    seq = nb * rows // batch

    def per_batch(a):
        return a.reshape((batch, seq) + a.shape[2:])

    fq_b, fk_f, fk_b, fv_f, fv_b, fg_f, sq_b, sk_f, sk_b, sv_f, sv_b, sg_f, ckv_f, kcat_b, mg_f, gout_f = [
        per_batch(a) for a in (fq_b, fk_f, fk_b, fv_f, fv_b, fg_f, sq_b, sk_f, sk_b, sv_f, sv_b, sg_f,
                               ckv_f, kcat_b, mg_f, gout_f)]
    qcat_b = jnp.transpose(qcat_b.reshape(nb, MLA_HEADS, batch // nb, seq, 256),
                           (0, 2, 1, 3, 4)).reshape(batch, MLA_HEADS, seq, 256)
    kpe_f = gout_f[:, :, :MLA_ROPE]
    logf = gout_f[:, :, MLA_ROPE:MLA_ROPE + N_FS_HEADS]
    new = (fk_f, fv_f, logf, sk_f, sv_f, ckv_f, kpe_f)

    n_kv = past + seq
    tq_fs, tk_fs = fs_tiles
    tq_m, tk_m = mla_tiles
    skv = -(-n_kv // max(tk_fs, tk_m)) * max(tk_fs, tk_m)
    if past_rows is None:
        fk_a, fv_a, sk_a, sv_a, kc_a, logf_a = fk_b, fv_b, sk_b, sv_b, kcat_b, logf
    else:
        c_fk, c_fv, c_logf, c_sk, c_sv, c_ckv, c_kpe = past_rows

        def join(c, n):
            c = c.reshape(batch, past, -1).astype(BF16)
            return _pad_rows(jnp.concatenate([c, n], axis=1), skv)

        fk_a, fv_a, sk_a, sv_a = join(c_fk, fk_b), join(c_fv, fv_b), join(c_sk, sk_b), join(c_sv, sv_b)
        c_kc = jnp.concatenate([c_ckv, c_kpe, jnp.ones((batch, past, 1), F32),
                                jnp.zeros((batch, past, LANES - MLA_ROPE - 1), F32)], axis=2)
        kc_a = join(c_kc, kcat_b)
        logf_a = jnp.concatenate([c_logf, logf], axis=1)
    frow, fcol = _forget_cumsum(logf_a, skv)

    if past_rows is None:
        y_fox = _fox_attention_t(fq_b, fk_a, fv_t, fcol, frow, fg_f, tq=tq_fs, tk=tk_fs)
        y_sb = _sb_attention_t(sq_b, sk_a, sv_t, sg_f, tq=tq_fs, tk=tk_fs)
        y_mla = _mla_attention_t(qcat_b, kc_a, ckv_t, lw["w_v"], mg_f, tq=tq_m, tk=tk_m)
    else:
        y_fox = _fox_attention(fq_b, fk_a, fv_a, fcol, frow, fg_f, past=past, tq=tq_fs, tk=tk_fs)
        y_sb = _sb_attention(sq_b, sk_a, sv_a, sg_f, past=past, tq=tq_fs, tk=tk_fs)
        y_mla = _mla_attention(qcat_b, kc_a, lw["w_v"], mg_f, past=past, n_kv=n_kv, tq=tq_m, tk=tk_m)

    def per_block(a):
        return a.reshape((nb, rows) + a.shape[2:])

    x_new = _output(per_block(y_fox), per_block(y_sb), per_block(y_mla), lw["w_out"], x, mod_rows,
                    lw["g_post"], tm)
    return x_new, new


def kernel(x_prompt, x_sample, c_prompt, c_sample, cache_fox_k, cache_fox_v, cache_fox_logf, cache_sb_k, cache_sb_v, cache_mla_ckv, cache_mla_kpe, g_pre, g_post, w_ada, b_ada, w_in, b_f, g_q_a, w_uq, g_kv_a, w_uk, w_uv, w_out):
    batch, seq, d = x_prompt.shape
    dec_batch, dec_seq, _ = x_sample.shape
    past_len = cache_fox_k.shape[2]
    dec_rows = dec_batch * dec_seq

    mod = _modulation(jnp.concatenate([c_prompt, c_sample], axis=0), w_ada, b_ada)
    cos_p, sin_p = _rope_tables(jnp.arange(seq, dtype=jnp.int32))
    pos_s = past_len + (jnp.arange(dec_rows, dtype=jnp.int32) % dec_seq)
    cos_s, sin_s = _rope_tables(pos_s)
    skv_s = -(-(past_len + dec_seq) // SB_SUB) * SB_SUB

    y_p = x_prompt
    y_s = x_sample.reshape(1, dec_rows, d)
    rows_p, rows_s = [], []
    for l in range(DEPTH):
        lw = _layer_weights(l, g_pre, g_post, w_in, b_f, g_q_a, w_uq, g_kv_a, w_uk, w_uv, w_out)
        mod_p = mod[l, :batch][:, None, :]
        mod_s = jnp.repeat(mod[l, batch:], dec_seq, axis=0)[None]
        y_p, new_p = _stream_layer(y_p, mod_p, cos_p, sin_p, lw, None, tm=512, past=0,
                                   fs_tiles=(256, 256), mla_tiles=(256, 512), batch=batch)
        past_rows = (cache_fox_k[l], cache_fox_v[l], cache_fox_logf[l], cache_sb_k[l], cache_sb_v[l],
                     cache_mla_ckv[l], cache_mla_kpe[l])
        y_s, new_s = _stream_layer(y_s, mod_s, cos_s, sin_s, lw, past_rows, tm=dec_rows, past=past_len,
                                   fs_tiles=(dec_seq, skv_s), mla_tiles=(dec_seq, skv_s),
                                   batch=dec_batch)
        rows_p.append(new_p)
        rows_s.append(new_s)

    def stack(rows, idx, shape_tail):
        a = jnp.stack([r[idx] for r in rows])
        return a.reshape(a.shape[:3] + shape_tail)

    heads = (N_FS_HEADS, HEAD_DIM)
    tails = (heads, heads, (N_FS_HEADS,), heads, heads, (KV_LORA,), (MLA_ROPE,))
    outs_p = [stack(rows_p, i, t) for i, t in enumerate(tails)]
    outs_s = [stack(rows_s, i, t) for i, t in enumerate(tails)]
    return (y_p, y_s.reshape(dec_batch, dec_seq, d), *outs_p, *outs_s)
```
